```python
import math
import jax
import jax.numpy as jnp
from jax import lax
import numpy as np

D_MODEL = 1024
BATCH = 4
SEQ = 4096
DEPTH = 1
DEC_BATCH = 128
DEC_SEQ = 4
PAST_LEN = 8192
PAGE_SIZE = 128

D_CONV = 512
CONV_WIDTH = 31
N_HEADS = 16
N_KV_HEADS = 4
HEAD_DIM = 64
GROUP = N_HEADS // N_KV_HEADS
CMP_BLOCK = 32
CMP_HIDDEN = 128
SEL_BLOCK = 64
SEL_TOPK = 16
WINDOW = 512
WIN_QBLOCK = 128
SEL_QBLOCK = 64
ROPE_THETA = 10000.0
FORCE_BONUS = 1.0e3
N_KEYS = 128
N_EXPERTS = N_KEYS * N_KEYS
PEER_HEADS = 8
PEER_TOPK = 16
PEER_KEY_DIM = 256
PEER_CHUNK = 256
PEER_V_SCALE = 0.25
NORM_EPS = 1e-6

F32 = jnp.float32
NEG_INF = -1.0e30
TINY = 1.0e-30
ATTN_SCALE = HEAD_DIM ** -0.5

C_GLU = 2 * D_CONV
C_Q = N_HEADS * HEAD_DIM
C_KV = 3 * 2 * N_KV_HEADS * HEAD_DIM
C_BG = 3 * N_HEADS
C_MG = 2 * D_MODEL
D_IN = C_GLU + C_Q + C_KV + C_BG + C_MG
SPLIT_AT = (C_GLU, C_GLU + C_Q, C_GLU + C_Q + C_KV, C_GLU + C_Q + C_KV + C_BG)

kernel_name = 'hybrid_conformer_nsa_peer_step'


def rmsnorm(x, g):
    xf = x.astype(F32)
    xf = xf * lax.rsqrt(jnp.mean(xf * xf, axis=-1, keepdims=True) + NORM_EPS)
    return (xf * g.astype(F32)).astype(x.dtype)


def layernorm(x, g, b):
    xf = x.astype(F32)
    xc = xf - jnp.mean(xf, axis=-1, keepdims=True)
    var = jnp.mean(xc * xc, axis=-1, keepdims=True)
    return (xc * lax.rsqrt(var + NORM_EPS) * g.astype(F32) + b.astype(F32)).astype(x.dtype)


def rope(x, pos):
    half = HEAD_DIM // 2
    inv = jnp.exp(-(2.0 * math.log(ROPE_THETA) / HEAD_DIM) * jnp.arange(half, dtype=F32))
    ang = pos.astype(F32)[:, None] * inv[None, :]
    shape = (1, pos.shape[0]) + (1,) * (x.ndim - 3) + (half,)
    cos = jnp.cos(ang).reshape(shape)
    sin = jnp.sin(ang).reshape(shape)
    xf = x.astype(F32)
    x1, x2 = xf[..., :half], xf[..., half:]
    return jnp.concatenate([x1 * cos - x2 * sin, x2 * cos + x1 * sin], axis=-1).astype(x.dtype)


def masked_softmax(s, mask):
    s = jnp.where(mask, s, NEG_INF)
    m = jnp.max(s, axis=-1, keepdims=True)
    e = jnp.where(mask, jnp.exp(s - m), 0.0)
    return e / jnp.maximum(jnp.sum(e, axis=-1, keepdims=True), TINY)


def to_chunks(a, n):
    return jnp.moveaxis(a.reshape((a.shape[0], n, a.shape[1] // n) + a.shape[2:]), 1, 0)


def from_chunks(a):
    a = jnp.moveaxis(a, 0, 1)
    return a.reshape((a.shape[0], a.shape[1] * a.shape[2]) + a.shape[3:])


def project(x, pos, p):
    n, t, _ = x.shape
    z = rmsnorm(x, p['g_mix']) @ p['w_in']
    z_glu, z_q, z_kv, z_bg, z_mg = jnp.split(z, SPLIT_AT, axis=-1)
    glu = z_glu[..., :D_CONV] * jax.nn.sigmoid(z_glu[..., D_CONV:])
    q = rope(rmsnorm(z_q.reshape(n, t, N_HEADS, HEAD_DIM), p['q_gain']), pos)
    q = q.reshape(n, t, N_KV_HEADS, GROUP, HEAD_DIM)
    kv = z_kv.reshape(n, t, 3, 2, N_KV_HEADS, HEAD_DIM)
    k = rope(rmsnorm(kv[:, :, :, 0], p['k_gain'][:, None, :]), pos)
    kv = jnp.stack([k, kv[:, :, :, 1]], axis=3)
    bg = jax.nn.sigmoid(z_bg).reshape(n, t, 3, N_KV_HEADS, GROUP)
    mg = jax.nn.sigmoid(z_mg).reshape(n, t, 2, D_MODEL)
    return glu, q, kv[:, :, 0], kv[:, :, 1], kv[:, :, 2], bg, mg


def conv_module(glu, buf, p):
    xin = jnp.concatenate([buf.astype(glu.dtype), glu], axis=1)
    y = lax.conv_general_dilated(xin, p['conv_w'][:, None, :].astype(xin.dtype), window_strides=(1,),
                                 padding='VALID', dimension_numbers=('NWC', 'WIO', 'NWC'),
                                 feature_group_count=D_CONV)
    y = jax.nn.silu(layernorm(y + p['conv_b'], p['conv_ln_g'], p['conv_ln_b']))
    return y @ p['w_conv_out'], xin[:, -(CONV_WIDTH - 1):]


def compress(kv, p):
    n, t = kv.shape[:2]
    nb = t // CMP_BLOCK
    blk = kv[:, :nb * CMP_BLOCK].reshape(n, nb, CMP_BLOCK, 2, N_KV_HEADS, HEAD_DIM)
    w1 = p['cmp_w1']
    pe_term = jnp.einsum('scd,scde->se', p['cmp_pe'], w1)
    hdn = jax.nn.gelu(jnp.einsum('nbcshd,scde->nbshe', blk, w1) + pe_term[:, None, :], approximate=False)
    out = jnp.einsum('nbshe,sed->nbshd', hdn, p['cmp_w2'])
    return rmsnorm(out[:, :, 0], p['k_gain'][0]), out[:, :, 1]


def cmp_attend(q, qpos, kc, vc):
    nc = kc.shape[1]
    s = jnp.einsum('nqhgd,nchd->nhgqc', q, kc).astype(F32) * ATTN_SCALE
    blk_end = (jnp.arange(nc) + 1) * CMP_BLOCK - 1
    pr = masked_softmax(s, blk_end[None, :] <= qpos[:, None])
    return jnp.einsum('nhgqc,nchd->nqhgd', pr.astype(vc.dtype), vc), pr


def select_blocks(pr, qpos, total_len):
    n_sel = -(-total_len // SEL_BLOCK)
    per = SEL_BLOCK // CMP_BLOCK
    imp = jnp.sum(pr, axis=2)
    imp = jnp.pad(imp, ((0, 0), (0, 0), (0, 0), (0, n_sel * per - imp.shape[-1])))
    imp = imp.reshape(imp.shape[:3] + (n_sel, per)).sum(-1)
    b = jnp.arange(n_sel)
    cur = (qpos // SEL_BLOCK)[:, None]
    forced = (b[None, :] == 0) | (b[None, :] == cur) | (b[None, :] == cur - 1)
    valid = b[None, :] * SEL_BLOCK <= qpos[:, None]
    score = jnp.where(valid, imp + jnp.where(forced, FORCE_BONUS, 0.0), NEG_INF)
    _, idx = lax.top_k(score, min(SEL_TOPK, n_sel))
    return jnp.transpose(idx, (0, 2, 1, 3))


def sel_attend(q, qpos, kvg, idx):
    n, qc, hk, k = idx.shape
    kpos = idx[..., None] * SEL_BLOCK + jnp.arange(SEL_BLOCK)
    mask = (kpos <= qpos[None, :, None, None, None]).reshape(n, qc, hk, 1, k * SEL_BLOCK)
    kg = kvg[..., 0, :].reshape(n, qc, hk, k * SEL_BLOCK, HEAD_DIM)
    vg = kvg[..., 1, :].reshape(n, qc, hk, k * SEL_BLOCK, HEAD_DIM)
    s = jnp.einsum('nqhgd,nqhjd->nqhgj', q, kg).astype(F32) * ATTN_SCALE
    pr = masked_softmax(s, mask)
    return jnp.einsum('nqhgj,nqhjd->nqhgd', pr.astype(q.dtype), vg)


def sel_from_prompt(q, pos, idx, skv):
    n, t = q.shape[:2]
    blocks = skv.reshape(n, t // SEL_BLOCK, SEL_BLOCK, 2, N_KV_HEADS, HEAD_DIM)
    bi = jnp.arange(n)[:, None, None, None]
    hi = jnp.arange(N_KV_HEADS)[None, None, :, None]

    def body(c):
        qc, pc, ic = c
        return sel_attend(qc, pc, blocks[bi, ic, :, :, hi], ic)

    nq = t // SEL_QBLOCK
    return from_chunks(lax.map(body, (to_chunks(q, nq), pos.reshape(nq, -1), to_chunks(idx, nq))))


def sel_from_pages(q, pos, idx, skv_new, pool, layer, page_table, past_len):
    n, t = q.shape[:2]
    bpp = pool.shape[2] // SEL_BLOCK
    n_past_blk = past_len // SEL_BLOCK
    n_new_blk = -(-t // SEL_BLOCK)
    new_blocks = jnp.pad(skv_new, ((0, 0), (0, n_new_blk * SEL_BLOCK - t), (0, 0), (0, 0), (0, 0)))
    new_blocks = new_blocks.reshape(n, n_new_blk, SEL_BLOCK, 2, N_KV_HEADS, HEAD_DIM)
    ni = jnp.arange(n)[:, None, None, None]
    hi4 = jnp.arange(N_KV_HEADS)[None, None, :, None]
    hi5 = hi4[..., None]
    offs = jnp.arange(SEL_BLOCK)

    def body(c):
        qc, pc, ic = c
        bp = jnp.minimum(ic, n_past_blk - 1)
        phys = page_table[ni, bp // bpp]
        rows = (bp % bpp)[..., None] * SEL_BLOCK + offs
        g_past = pool[layer, phys[..., None], rows, :, hi5]
        g_new = new_blocks[ni, jnp.clip(ic - n_past_blk, 0, n_new_blk - 1), :, :, hi4]
        g = jnp.where((ic >= n_past_blk)[..., None, None, None], g_new, g_past)
        return sel_attend(qc, pc, g, ic)

    return from_chunks(lax.map(body, (to_chunks(q, t), pos.reshape(t, 1), to_chunks(idx, t))))


def win_attend(q, qpos, kv, kpos):
    s = jnp.einsum('nqhgd,nkhd->nhgqk', q, kv[:, :, 0]).astype(F32) * ATTN_SCALE
    diff = qpos[:, None] - kpos[None, :]
    mask = (diff >= 0) & (diff < WINDOW) & (kpos[None, :] >= 0)
    pr = masked_softmax(s, mask)
    return jnp.einsum('nhgqk,nkhd->nqhgd', pr.astype(q.dtype), kv[:, :, 1])


def win_from_prompt(q, pos, wkv):
    n, t = q.shape[:2]
    nbq = t // WIN_QBLOCK
    nband = WINDOW // WIN_QBLOCK + 1
    kvp = jnp.pad(wkv, ((0, 0), (WINDOW, 0), (0, 0), (0, 0), (0, 0)))
    kvp = kvp.reshape((n, (t + WINDOW) // WIN_QBLOCK, WIN_QBLOCK) + wkv.shape[2:])
    band = jnp.concatenate([kvp[:, j:j + nbq] for j in range(nband)], axis=2)
    kpos = jnp.arange(nbq)[:, None] * WIN_QBLOCK - WINDOW + jnp.arange(nband * WIN_QBLOCK)[None, :]

    def body(c):
        qc, pc, bc, kp = c
        return win_attend(qc, pc, bc, kp)

    out = from_chunks(lax.map(body, (to_chunks(q, nbq), pos.reshape(nbq, -1), jnp.moveaxis(band, 1, 0), kpos)))
    return out, wkv[:, -min(WINDOW, t):]


def win_from_buffer(q, pos, wkv, buf, past_len):
    lb = buf.shape[1]
    kv = jnp.concatenate([buf.astype(wkv.dtype), wkv], axis=1)
    kpos = past_len - lb + jnp.arange(lb + wkv.shape[1])
    return win_attend(q, pos, kv, kpos), kv[:, -lb:]


def peer(xn, p):
    n, d = xn.shape
    xc = jnp.pad(xn, ((0, -n % PEER_CHUNK), (0, 0))).reshape(-1, PEER_CHUNK, d)
    half = PEER_KEY_DIM // 2

    def body(xb):
        qh = (xb @ p['peer_wq']).reshape(PEER_CHUNK, PEER_HEADS, 2, half)
        s = jnp.einsum('chsd,hskd->chsk', qh, p['peer_subkeys']).astype(F32)
        va, ia = lax.top_k(s[:, :, 0], PEER_TOPK)
        vb, ib = lax.top_k(s[:, :, 1], PEER_TOPK)
        cand = (va[..., :, None] + vb[..., None, :]).reshape(PEER_CHUNK, PEER_HEADS, PEER_TOPK * PEER_TOPK)
        sc, ci = lax.top_k(cand, PEER_TOPK)
        e = (jnp.take_along_axis(ia, ci // PEER_TOPK, axis=-1) * N_KEYS
             + jnp.take_along_axis(ib, ci % PEER_TOPK, axis=-1))
        g = jax.nn.softmax(sc, axis=-1)
        u = p['peer_u'][e]
        v = p['peer_v'][e]
        act = jax.nn.gelu(jnp.einsum('cd,chkd->chk', xb, u).astype(F32), approximate=False)
        return jnp.einsum('chk,chkd->cd', (g * act).astype(xb.dtype), v)

    return lax.map(body, xc).reshape(-1, d)[:n]


def layer_step(x, pos, p, past):
    n, t, _ = x.shape
    glu, q, ckv, skv, wkv, bg, mg = project(x, pos, p)
    if past is None:
        conv_out, conv_state = conv_module(glu, jnp.zeros((n, CONV_WIDTH - 1, D_CONV), glu.dtype), p)
        kc, vc = compress(ckv, p)
        o_cmp, pr = cmp_attend(q, pos, kc, vc)
        idx = select_blocks(pr, pos, t)
        o_sel = sel_from_prompt(q, pos, idx, skv)
        o_win, win_state = win_from_prompt(q, pos, wkv)
    else:
        past_len = past['len']
        conv_out, conv_state = conv_module(glu, past['conv'], p)
        old = past['cmp_pool'][past['layer'], past['page_table']]
        old = old.reshape((n, past_len) + old.shape[3:])
        kc_old, vc_old = compress(old, p)
        kc_new, vc_new = compress(ckv, p)
        kc = jnp.concatenate([kc_old, kc_new], axis=1)
        vc = jnp.concatenate([vc_old, vc_new], axis=1)
        o_cmp, pr = cmp_attend(q, pos, kc, vc)
        idx = select_blocks(pr, pos, past_len + t)
        o_sel = sel_from_pages(q, pos, idx, skv, past['sel_pool'], past['layer'], past['page_table'], past_len)
        o_win, win_state = win_from_buffer(q, pos, wkv, past['win'], past_len)
    o = (bg[:, :, 0, :, :, None] * o_cmp + bg[:, :, 1, :, :, None] * o_sel
         + bg[:, :, 2, :, :, None] * o_win)
    attn_out = o.reshape(n, t, N_HEADS * HEAD_DIM) @ p['w_attn_out']
    h = x + (mg[:, :, 0] * conv_out + mg[:, :, 1] * attn_out) @ p['w_o']
    y = h + peer(rmsnorm(h, p['g_ffn']).reshape(n * t, D_MODEL), p).reshape(n, t, D_MODEL)
    return y, (ckv, skv, win_state, conv_state)


def setup_inputs(seed: int = 0) -> dict:
    key = jax.random.key(seed)
    ks = jax.random.split(key, 26)
    n_pages = PAST_LEN // PAGE_SIZE
    n_used = DEC_BATCH * n_pages
    n_pool = (5 * n_used + 3) // 4
    win_buf = min(WINDOW, PAST_LEN)
    pool_shape = (DEPTH, n_pool, PAGE_SIZE, 2, N_KV_HEADS, HEAD_DIM)

    def nrm(k, shape, scale):
        return scale * jax.random.normal(k, shape, F32)

    def gain(k, shape):
        return 1.0 + 0.02 * jax.random.normal(k, shape, F32)

    return {
        'x_prompt': jax.random.normal(ks[0], (BATCH, SEQ, D_MODEL), F32),
        'x_sample': jax.random.normal(ks[1], (DEC_BATCH, DEC_SEQ, D_MODEL), F32),
        'cache_cmp_kv': jax.random.normal(ks[2], pool_shape, F32),
        'cache_sel_kv': jax.random.normal(ks[3], pool_shape, F32),
        'state_win_kv': jax.random.normal(ks[4], (DEPTH, DEC_BATCH, win_buf, 2, N_KV_HEADS, HEAD_DIM), F32),
        'state_conv': nrm(ks[5], (DEPTH, DEC_BATCH, CONV_WIDTH - 1, D_CONV), 0.5),
        'page_table': jax.random.permutation(ks[6], n_pool)[:n_used].reshape(DEC_BATCH, n_pages).astype(jnp.int32),
        'g_mix': gain(ks[7], (DEPTH, D_MODEL)),
        'w_in': nrm(ks[8], (DEPTH, D_MODEL, D_IN), D_MODEL ** -0.5),
        'conv_w': nrm(ks[9], (DEPTH, CONV_WIDTH, D_CONV), CONV_WIDTH ** -0.5),
        'conv_b': nrm(ks[10], (DEPTH, D_CONV), 0.02),
        'conv_ln_g': gain(ks[11], (DEPTH, D_CONV)),
        'conv_ln_b': nrm(ks[12], (DEPTH, D_CONV), 0.02),
        'w_conv_out': nrm(ks[13], (DEPTH, D_CONV, D_MODEL), D_CONV ** -0.5),
        'q_gain': gain(ks[14], (DEPTH, HEAD_DIM)),
        'k_gain': gain(ks[15], (DEPTH, 3, HEAD_DIM)),
        'cmp_pe': nrm(ks[16], (DEPTH, 2, CMP_BLOCK, HEAD_DIM), 0.5),
        'cmp_w1': nrm(ks[17], (DEPTH, 2, CMP_BLOCK, HEAD_DIM, CMP_HIDDEN), (CMP_BLOCK * HEAD_DIM) ** -0.5),
        'cmp_w2': nrm(ks[18], (DEPTH, 2, CMP_HIDDEN, HEAD_DIM), CMP_HIDDEN ** -0.5),
        'w_attn_out': nrm(ks[19], (DEPTH, N_HEADS * HEAD_DIM, D_MODEL), (N_HEADS * HEAD_DIM) ** -0.5),
        'w_o': nrm(ks[20], (DEPTH, D_MODEL, D_MODEL), D_MODEL ** -0.5),
        'g_ffn': gain(ks[21], (DEPTH, D_MODEL)),
        'peer_wq': nrm(ks[22], (DEPTH, D_MODEL, PEER_HEADS * PEER_KEY_DIM), D_MODEL ** -0.5),
        'peer_subkeys': nrm(ks[23], (DEPTH, PEER_HEADS, 2, N_KEYS, PEER_KEY_DIM // 2), (PEER_KEY_DIM // 2) ** -0.5),
        'peer_u': nrm(ks[24], (DEPTH, N_EXPERTS, D_MODEL), D_MODEL ** -0.5),
        'peer_v': nrm(ks[25], (DEPTH, N_EXPERTS, D_MODEL), PEER_V_SCALE),
    }


def reference(x_prompt, x_sample, cache_cmp_kv, cache_sel_kv, state_win_kv, state_conv, page_table,
              g_mix, w_in, conv_w, conv_b, conv_ln_g, conv_ln_b, w_conv_out, q_gain, k_gain,
              cmp_pe, cmp_w1, cmp_w2, w_attn_out, w_o, g_ffn, peer_wq, peer_subkeys, peer_u, peer_v):
    seq = x_prompt.shape[1]
    dec_seq = x_sample.shape[1]
    past_len = page_table.shape[1] * cache_cmp_kv.shape[2]
    pos_p = jnp.arange(seq, dtype=jnp.int32)
    pos_s = past_len + jnp.arange(dec_seq, dtype=jnp.int32)
    yp, ys = x_prompt, x_sample
    st_p = ([], [], [], [])
    st_s = ([], [], [], [])
    for l in range(DEPTH):
        p = {'g_mix': g_mix[l], 'w_in': w_in[l], 'conv_w': conv_w[l], 'conv_b': conv_b[l],
             'conv_ln_g': conv_ln_g[l], 'conv_ln_b': conv_ln_b[l], 'w_conv_out': w_conv_out[l],
             'q_gain': q_gain[l], 'k_gain': k_gain[l], 'cmp_pe': cmp_pe[l], 'cmp_w1': cmp_w1[l],
             'cmp_w2': cmp_w2[l], 'w_attn_out': w_attn_out[l], 'w_o': w_o[l], 'g_ffn': g_ffn[l],
             'peer_wq': peer_wq[l], 'peer_subkeys': peer_subkeys[l], 'peer_u': peer_u[l], 'peer_v': peer_v[l]}
        yp, sp = layer_step(yp, pos_p, p, None)
        past = {'layer': l, 'len': past_len, 'page_table': page_table, 'cmp_pool': cache_cmp_kv,
                'sel_pool': cache_sel_kv, 'win': state_win_kv[l], 'conv': state_conv[l]}
        ys, ss = layer_step(ys, pos_s, p, past)
        for lst, a in zip(st_p, sp):
            lst.append(a)
        for lst, a in zip(st_s, ss):
            lst.append(a)
    new_cmp_p = jnp.stack(st_p[0])
    new_sel_p = jnp.stack(st_p[1])
    new_win_p = jnp.stack(st_p[2])
    new_conv_p = jnp.stack(st_p[3])
    new_cmp_s = jnp.stack(st_s[0])
    new_sel_s = jnp.stack(st_s[1])
    new_win_s = jnp.stack(st_s[2])
    new_conv_s = jnp.stack(st_s[3])
    return (yp, ys, new_cmp_p, new_sel_p, new_win_p, new_conv_p, new_cmp_s, new_sel_s, new_win_s, new_conv_s)
```

```python
import functools
import math

import jax
import jax.numpy as jnp
from jax import lax
from jax.experimental import pallas as pl
from jax.experimental.pallas import tpu as pltpu

F32 = jnp.float32
BF16 = jnp.bfloat16

D_MODEL = 1024
D_CONV = 512
CONV_WIDTH = 31
N_HEADS = 16
N_KV_HEADS = 4
HEAD_DIM = 64
GROUP = N_HEADS // N_KV_HEADS
CMP_BLOCK = 32
CMP_HIDDEN = 128
SEL_BLOCK = 64
SEL_TOPK = 16
WINDOW = 512
ROPE_THETA = 10000.0
FORCE_BONUS = 1.0e3
N_KEYS = 128
PEER_HEADS = 8
PEER_TOPK = 16
PEER_KEY_DIM = 256
NORM_EPS = 1e-6
NEG_INF = -1.0e30
TINY = 1.0e-30
ATTN_SCALE = HEAD_DIM ** -0.5
KV_ROW = 2 * N_KV_HEADS * HEAD_DIM
C_Q = N_HEADS * HEAD_DIM

ROW_TILE = 256
LANES = 128
VMEM_LIMIT = 56 * 1024 * 1024


def _dot(a, b):
    return jnp.dot(a, b, preferred_element_type=F32)


def _dot_nt(a, b):
    return lax.dot_general(a, b, (((1,), (1,)), ((), ())), preferred_element_type=F32)


def _dot_tn(a, b):
    return lax.dot_general(a, b, (((0,), (0,)), ((), ())), preferred_element_type=F32)


def _split_bf16(x):
    hi = x.astype(BF16)
    lo = (x - hi.astype(F32)).astype(BF16)
    return hi, lo


def _sigmoid(x):
    return 1.0 / (1.0 + jnp.exp(-x))


def _params(sem):
    return pltpu.CompilerParams(dimension_semantics=sem, vmem_limit_bytes=VMEM_LIMIT)


def _const_spec(shape):
    nd = len(shape)
    return pl.BlockSpec(shape, lambda *_: (0,) * nd)


def _head_pair_ones():
    r = lax.broadcasted_iota(jnp.int32, (LANES, LANES), 0) // HEAD_DIM
    c = lax.broadcasted_iota(jnp.int32, (LANES, LANES), 1) // HEAD_DIM
    return jnp.where(r == c, 1.0, 0.0).astype(BF16)


def _head_rms_rope(z, gain, cos, sin_signed, ones_bd, first_half):
    hi, lo = _split_bf16(z * z)
    ss = _dot(hi, ones_bd) + _dot(lo, ones_bd)
    zn = z * lax.rsqrt(ss * (1.0 / HEAD_DIM) + NORM_EPS) * gain
    partner = jnp.where(first_half, pltpu.roll(zn, LANES - HEAD_DIM // 2, 1), pltpu.roll(zn, HEAD_DIM // 2, 1))
    return zn * cos + partner * sin_signed


def _proj_kernel(x_ref, cos_ref, sin_ref, gmix_ref, wglu_ref, wq_ref, wkv_ref, wbg_ref, qg_ref, kg_ref,
                 glu_ref, q_ref, ckv_ref, skv_ref, wkvo_ref, bg_ref, *dup_refs):
    x = x_ref[...]
    ms = jnp.mean(x * x, axis=-1, keepdims=True)
    n = (x * lax.rsqrt(ms + NORM_EPS) * gmix_ref[...]).astype(BF16)
    cos = cos_ref[...]
    sin_signed = sin_ref[...]
    ones_bd = _head_pair_ones()
    lane = lax.broadcasted_iota(jnp.int32, (1, LANES), 1)
    first_half = (lane % HEAD_DIM) < (HEAD_DIM // 2)

    zg = _dot(n, wglu_ref[...])
    glu_ref[...] = zg[:, :D_CONV] * _sigmoid(zg[:, D_CONV:])

    zq = _dot(n, wq_ref[...])
    qg = qg_ref[...]
    for j in range(C_Q // LANES):
        sl = slice(j * LANES, (j + 1) * LANES)
        qj = _head_rms_rope(zq[:, sl], qg, cos, sin_signed, ones_bd, first_half)
        q_ref[:, sl] = (qj * ATTN_SCALE).astype(BF16)

    zkv = _dot(n, wkv_ref[...])
    half = KV_ROW // 2
    lower = lane < HEAD_DIM
    for b, out_ref in enumerate((ckv_ref, skv_ref, wkvo_ref)):
        kg = kg_ref[b:b + 1, :]
        for j in range(half // LANES):
            sl = slice(b * KV_ROW + j * LANES, b * KV_ROW + (j + 1) * LANES)
            kj = _head_rms_rope(zkv[:, sl], kg, cos, sin_signed, ones_bd, first_half)
            out_ref[:, j * LANES:(j + 1) * LANES] = kj
            vj = zkv[:, b * KV_ROW + half + j * LANES:b * KV_ROW + half + (j + 1) * LANES]
            if b > 0:
                for src, dup_ref in ((kj, dup_refs[2 * (b - 1)]), (vj, dup_refs[2 * (b - 1) + 1])):
                    swapped = pltpu.roll(src, HEAD_DIM, 1)
                    dup_ref[:, 2 * j * LANES:(2 * j + 1) * LANES] = jnp.where(lower, src, swapped).astype(BF16)
                    dup_ref[:, (2 * j + 1) * LANES:(2 * j + 2) * LANES] = jnp.where(lower, swapped, src).astype(BF16)
        out_ref[:, half:] = zkv[:, b * KV_ROW + half:(b + 1) * KV_ROW]

    bg_ref[...] = _sigmoid(_dot(n, wbg_ref[...]))


def _rope_tables(pos):
    half = HEAD_DIM // 2
    inv = jnp.exp(-(2.0 * math.log(ROPE_THETA) / HEAD_DIM) * jnp.arange(half, dtype=F32))
    ang = pos.astype(F32)[:, None] * inv[None, :]
    lane = jnp.arange(LANES)
    cos = jnp.cos(ang)[:, lane % half]
    sin = jnp.sin(ang)[:, lane % half]
    sin_signed = jnp.where((lane % HEAD_DIM) < half, -sin, sin)
    return cos, sin_signed


def _project(x2d, cos, sin_signed, w, row_tile):
    ntok = x2d.shape[0]
    n_pos_blocks = cos.shape[0] // row_tile
    row = lambda width: pl.BlockSpec((row_tile, width), lambda i: (i, 0))
    tab = pl.BlockSpec((row_tile, LANES), lambda i: (i % n_pos_blocks, 0))
    consts = (w['g_mix'], w['w_glu'], w['w_q'], w['w_kv'], w['w_bgx'], w['q_gain2'], w['k_gain2'])
    return pl.pallas_call(
        _proj_kernel,
        grid=(ntok // row_tile,),
        in_specs=[row(D_MODEL), tab, tab] + [_const_spec(c.shape) for c in consts],
        out_specs=[row(D_CONV), row(C_Q), row(KV_ROW), row(KV_ROW), row(KV_ROW), row(3 * C_Q)] + [row(KV_ROW)] * 4,
        out_shape=[jax.ShapeDtypeStruct((ntok, D_CONV), F32), jax.ShapeDtypeStruct((ntok, C_Q), BF16),
                   jax.ShapeDtypeStruct((ntok, KV_ROW), F32), jax.ShapeDtypeStruct((ntok, KV_ROW), F32),
                   jax.ShapeDtypeStruct((ntok, KV_ROW), F32), jax.ShapeDtypeStruct((ntok, 3 * C_Q), F32)]
        + [jax.ShapeDtypeStruct((ntok, KV_ROW), BF16)] * 4,
        compiler_params=_params(("parallel",)),
        name="proj",
    )(x2d, cos, sin_signed, *consts)


def _prep_weights(p):
    w_in = p['w_in']
    c_glu = 2 * D_CONV
    c_kv = 3 * KV_ROW
    c_bg = 3 * N_HEADS
    o1, o2, o3 = c_glu, c_glu + C_Q, c_glu + C_Q + c_kv
    o4 = o3 + c_bg
    row = lambda v: v.reshape(1, -1)
    return {
        'g_mix': row(p['g_mix']),
        'w_glu': w_in[:, :o1].astype(BF16),
        'w_q': w_in[:, o1:o2].astype(BF16),
        'w_kv': w_in[:, o2:o3].astype(BF16),
        'w_bgx': jnp.repeat(w_in[:, o3:o4], HEAD_DIM, axis=1).astype(BF16),
        'w_mg': w_in[:, o4:].astype(BF16),
        'q_gain2': row(jnp.tile(p['q_gain'], 2)),
        'k_gain2': jnp.tile(p['k_gain'], (1, 2)),
        'conv_w': p['conv_w'],
        'conv_b': row(p['conv_b']),
        'conv_ln_g': row(p['conv_ln_g']),
        'conv_ln_b': row(p['conv_ln_b']),
        'w_conv_out': p['w_conv_out'].astype(BF16),
        'cmp_w1bd': _pair_block_diag(p['cmp_w1']).astype(BF16),
        'cmp_w2bd': _pair_block_diag(p['cmp_w2']).astype(BF16),
        'w_attn_out': p['w_attn_out'].astype(BF16),
        'w_o': p['w_o'].astype(BF16),
        'g_ffn': row(p['g_ffn']),
        'peer_wq': p['peer_wq'].astype(BF16),
        'peer_subkeys': p['peer_subkeys'].reshape(2 * PEER_HEADS, N_KEYS, PEER_KEY_DIM // 2).astype(BF16),
        'cmp_pe2': jnp.tile(jnp.einsum('scd,scde->se', p['cmp_pe'], p['cmp_w1'], precision=lax.Precision.HIGHEST), (1, 2)),
    }


def _pair_block_diag(a):
    z = jnp.zeros_like(a)
    return jnp.concatenate([jnp.concatenate([a, z], axis=-1), jnp.concatenate([z, a], axis=-1)], axis=-2)


CONV_HALO = 32
CONV_CHUNK = 64


def _ln_silu_project(y, lng, lnb, wout):
    mu = jnp.mean(y, axis=-1, keepdims=True)
    yc = y - mu
    var = jnp.mean(yc * yc, axis=-1, keepdims=True)
    yn = yc * lax.rsqrt(var + NORM_EPS) * lng + lnb
    act = yn * _sigmoid(yn)
    return _dot(act.astype(BF16), wout)


def _conv_prompt_kernel(glu_ref, cw_ref, cb_ref, lng_ref, lnb_ref, wout_ref, out_ref, ext_ref, y_ref):
    t = pl.program_id(1)
    tt = glu_ref.shape[1]

    @pl.when(t == 0)
    def _():
        ext_ref[0:CONV_HALO, :] = jnp.zeros((CONV_HALO, D_CONV), F32)

    @pl.when(t > 0)
    def _():
        ext_ref[0:CONV_HALO, :] = ext_ref[tt:tt + CONV_HALO, :]

    ext_ref[CONV_HALO:CONV_HALO + tt, :] = glu_ref[0]
    off = CONV_HALO - (CONV_WIDTH - 1)
    for r0 in range(0, tt, CONV_CHUNK):
        acc = jnp.zeros((CONV_CHUNK, D_CONV), F32)
        for j in range(CONV_WIDTH):
            acc = acc + ext_ref[r0 + off + j:r0 + off + j + CONV_CHUNK, :] * cw_ref[j:j + 1, :]
        y_ref[r0:r0 + CONV_CHUNK, :] = acc + cb_ref[...]
    out_ref[0] = _ln_silu_project(y_ref[...], lng_ref[...], lnb_ref[...], wout_ref[...])


def _conv_prompt(glu, w, t_tile):
    n, t, _ = glu.shape
    consts = (w['conv_w'], w['conv_b'], w['conv_ln_g'], w['conv_ln_b'], w['w_conv_out'])
    return pl.pallas_call(
        _conv_prompt_kernel,
        grid=(n, t // t_tile),
        in_specs=[pl.BlockSpec((1, t_tile, D_CONV), lambda i, j: (i, j, 0))] + [_const_spec(c.shape) for c in consts],
        out_specs=pl.BlockSpec((1, t_tile, D_MODEL), lambda i, j: (i, j, 0)),
        out_shape=jax.ShapeDtypeStruct((n, t, D_MODEL), F32),
        scratch_shapes=[pltpu.VMEM((t_tile + CONV_HALO, D_CONV), F32), pltpu.VMEM((t_tile, D_CONV), F32)],
        compiler_params=_params(("parallel", "arbitrary")),
        name="conv_prompt",
    )(glu, *consts)


def _conv_step_kernel(st_ref, gl_ref, cw_ref, cb_ref, lng_ref, lnb_ref, wout_ref, out_ref):
    n_buf = st_ref.shape[0]
    for t in range(gl_ref.shape[0]):
        acc = jnp.zeros(gl_ref.shape[1:], F32)
        for j in range(CONV_WIDTH):
            k = t + j
            row = st_ref[k] if k < n_buf else gl_ref[k - n_buf]
            acc = acc + row * cw_ref[j:j + 1, :]
        out_ref[t] = _ln_silu_project(acc + cb_ref[...], lng_ref[...], lnb_ref[...], wout_ref[...])


def _conv_step(st_tm, gl_tm, w):
    t, b, _ = gl_tm.shape
    consts = (w['conv_w'], w['conv_b'], w['conv_ln_g'], w['conv_ln_b'], w['w_conv_out'])
    return pl.pallas_call(
        _conv_step_kernel,
        grid=(1,),
        in_specs=[_const_spec(st_tm.shape), _const_spec(gl_tm.shape)] + [_const_spec(c.shape) for c in consts],
        out_specs=_const_spec((t, b, D_MODEL)),
        out_shape=jax.ShapeDtypeStruct((t, b, D_MODEL), F32),
        compiler_params=_params(("arbitrary",)),
        name="conv_step",
    )(st_tm, gl_tm, *consts)


BLOCK_COLS = CMP_BLOCK * KV_ROW
PAGE_SIZE = 128
PAGE_BLOCKS = PAGE_SIZE // CMP_BLOCK


def _gelu(x):
    return 0.5 * x * (1.0 + lax.erf(x * (2.0 ** -0.5)))


def _compress_rows(x_ref, m, w1_ref, pe_ref, w2_ref, kg_ref):
    outs = []
    for s in range(2):
        parts = []
        for hp in range(N_KV_HEADS // 2):
            acc = jnp.zeros((m, 2 * CMP_HIDDEN), F32)
            for c in range(CMP_BLOCK):
                col = c * KV_ROW + s * (KV_ROW // 2) + hp * LANES
                xs = x_ref[:, pl.ds(col, LANES)]
                acc = acc + _dot(xs.astype(BF16), w1_ref[s, c])
            hdn = _gelu(acc + pe_ref[s:s + 1, :])
            parts.append(_dot(hdn.astype(BF16), w2_ref[s]))
        outs.append(parts)
    ones_bd = _head_pair_ones()
    kc = []
    for part in outs[0]:
        hi, lo = _split_bf16(part * part)
        ss = _dot(hi, ones_bd) + _dot(lo, ones_bd)
        kc.append(part * lax.rsqrt(ss * (1.0 / HEAD_DIM) + NORM_EPS) * kg_ref[...])
    return jnp.concatenate(kc, axis=1), jnp.concatenate(outs[1], axis=1)


def _compress_prompt_kernel(x_ref, w1_ref, pe_ref, w2_ref, kg_ref, kc_ref, vc_ref):
    kc, vc = _compress_rows(x_ref, kc_ref.shape[0], w1_ref, pe_ref, w2_ref, kg_ref)
    kc_ref[...] = kc
    vc_ref[...] = vc


def _compress_prompt(ckv2d, w, m_tile):
    nblk = ckv2d.shape[0] // CMP_BLOCK
    x = ckv2d.reshape(nblk, BLOCK_COLS)
    consts = (w['cmp_w1bd'], w['cmp_pe2'], w['cmp_w2bd'], w['k_gain2'][0:1])
    out = pl.BlockSpec((m_tile, KV_ROW // 2), lambda i: (i, 0))
    return pl.pallas_call(
        _compress_prompt_kernel,
        grid=(nblk // m_tile,),
        in_specs=[pl.BlockSpec((m_tile, BLOCK_COLS), lambda i: (i, 0))] + [_const_spec(c.shape) for c in consts],
        out_specs=[out, out],
        out_shape=[jax.ShapeDtypeStruct((nblk, KV_ROW // 2), F32)] * 2,
        compiler_params=_params(("parallel",)),
        name="compress_prompt",
    )(x, *consts)


def _page_copy(pool_ref, pt_ref, buf_ref, sem, seq, i):
    return pltpu.make_async_copy(pool_ref.at[pt_ref[seq, i]],
                                 buf_ref.at[pl.ds(i * PAGE_BLOCKS, PAGE_BLOCKS), :], sem)


def _compress_pages_kernel(pt_ref, pool_ref, w1_ref, pe_ref, w2_ref, kg_ref, kc_ref, vc_ref, buf_ref, sem):
    seq = pl.program_id(0)
    n_pages = pt_ref.shape[1]
    for i in range(n_pages):
        _page_copy(pool_ref, pt_ref, buf_ref, sem, seq, i).start()
    for i in range(n_pages):
        _page_copy(pool_ref, pt_ref, buf_ref, sem, seq, i).wait()
    kc, vc = _compress_rows(buf_ref, kc_ref.shape[0], w1_ref, pe_ref, w2_ref, kg_ref)
    kc_ref[...] = kc
    vc_ref[...] = vc


def _compress_pages(pool, page_table, w):
    n_seq, n_pages = page_table.shape
    pool2 = pool.reshape(pool.shape[0], PAGE_BLOCKS, BLOCK_COLS)
    m = n_pages * PAGE_SIZE // CMP_BLOCK
    consts = (w['cmp_w1bd'], w['cmp_pe2'], w['cmp_w2bd'], w['k_gain2'][0:1])
    out = pl.BlockSpec((m, KV_ROW // 2), lambda i, pt: (i, 0))
    grid_spec = pltpu.PrefetchScalarGridSpec(
        num_scalar_prefetch=1,
        grid=(n_seq,),
        in_specs=[pl.BlockSpec(memory_space=pl.ANY)]
        + [pl.BlockSpec(c.shape, functools.partial(lambda nd, i, pt: (0,) * nd, len(c.shape))) for c in consts],
        out_specs=[out, out],
        scratch_shapes=[pltpu.VMEM((m, BLOCK_COLS), F32), pltpu.SemaphoreType.DMA(())],
    )
    return pl.pallas_call(
        _compress_pages_kernel,
        grid_spec=grid_spec,
        out_shape=[jax.ShapeDtypeStruct((n_seq * m, KV_ROW // 2), F32)] * 2,
        compiler_params=_params(("arbitrary",)),
        name="compress_pages",
    )(page_table, pool2, *consts)


NSA_TQ = 256
CMP_PER_SEL = SEL_BLOCK // CMP_BLOCK
SEL_PER_TILE = NSA_TQ // SEL_BLOCK
WIN_TILES = WINDOW // NSA_TQ + 1


def _dup_halves(x, in_head_half):
    return jnp.where(in_head_half, x, pltpu.roll(x, HEAD_DIM, 1))


def _rank_select(score, n_rows):
    row = lax.broadcasted_iota(jnp.int32, (n_rows, 1), 0)
    rank = jnp.zeros(score.shape, F32)
    for b in range(n_rows):
        sb = score[b:b + 1, :]
        tie_wins = jnp.where(row > b, 1.0, 0.0)
        rank = rank + jnp.where(sb > score, 1.0, jnp.where(sb == score, tie_wins, 0.0))
    return jnp.where(rank < SEL_TOPK, 1.0, 0.0)


def _flash_tile(qm, kd, v_lo, v_hi, mask, m, l, acc):
    lower = lax.broadcasted_iota(jnp.int32, (1, LANES), 1) < HEAD_DIM
    new_m, new_l, alphas, pvs = [], [], [], []
    for g in range(GROUP):
        s = jnp.where(mask, _dot_nt(qm[g], kd), NEG_INF)
        mg = jnp.maximum(m[g], jnp.max(s, axis=-1, keepdims=True))
        alpha = jnp.exp(m[g] - mg)
        p = jnp.exp(s - mg)
        new_l.append(alpha * l[g] + jnp.sum(p, axis=-1, keepdims=True))
        new_m.append(mg)
        alphas.append(alpha)
        pvs.append(_dot(p.astype(BF16), v_lo if g % 2 == 0 else v_hi))
    new_acc = [acc[j] * jnp.where(lower, alphas[2 * j], alphas[2 * j + 1]) + pvs[2 * j] + pvs[2 * j + 1]
               for j in range(GROUP // 2)]
    return new_m, new_l, new_acc


def _nsa_prompt_kernel(q_ref, kc_ref, vc_ref, ks_ref, vs_ref, kw0_ref, kw1_ref, kw2_ref, vw0_ref, vw1_ref, vw2_ref,
                       bg0_ref, bg1_ref, bg2_ref, o_ref):
    qi = pl.program_id(1)
    h = pl.program_id(2)
    tq = q_ref.shape[0]
    lane = lax.broadcasted_iota(jnp.int32, (1, LANES), 1)
    lower = lane < HEAD_DIM
    upper = lane >= HEAD_DIM
    in_head_half = (lane // HEAD_DIM) == (h % 2)
    pos = qi * tq + lax.broadcasted_iota(jnp.int32, (tq, 1), 0)

    q = q_ref[...]
    zero = jnp.zeros((tq, LANES), BF16)
    qm = [jnp.where(lower if g % 2 == 0 else upper, q[:, (g // 2) * LANES:(g // 2 + 1) * LANES], zero)
          for g in range(GROUP)]

    kcd = _dup_halves(kc_ref[...], in_head_half).astype(BF16)
    vcd = _dup_halves(vc_ref[...], in_head_half)
    vc_lo = jnp.where(lower, vcd, 0.0).astype(BF16)
    vc_hi = jnp.where(lower, 0.0, vcd).astype(BF16)
    nc = kc_ref.shape[0]
    blk_end = (lax.broadcasted_iota(jnp.int32, (1, nc), 1) + 1) * CMP_BLOCK - 1
    cmask = blk_end <= pos
    imp = jnp.zeros((tq, nc), F32)
    o_cmp = [jnp.zeros((tq, LANES), F32) for _ in range(GROUP // 2)]
    for g in range(GROUP):
        s = jnp.where(cmask, _dot_nt(qm[g], kcd), NEG_INF)
        e = jnp.where(cmask, jnp.exp(s - jnp.max(s, axis=-1, keepdims=True)), 0.0)
        pr = e / jnp.maximum(jnp.sum(e, axis=-1, keepdims=True), TINY)
        imp = imp + pr
        o_cmp[g // 2] = o_cmp[g // 2] + _dot(pr.astype(BF16), vc_lo if g % 2 == 0 else vc_hi)

    n_sel = nc // CMP_PER_SEL
    pair_t = jnp.where(lax.broadcasted_iota(jnp.int32, (n_sel, nc), 1) // CMP_PER_SEL
                       == lax.broadcasted_iota(jnp.int32, (n_sel, nc), 0), 1.0, 0.0).astype(BF16)
    imp_hi, imp_lo = _split_bf16(imp)
    imp_t = _dot_nt(pair_t, imp_hi) + _dot_nt(pair_t, imp_lo)
    pos_t = qi * tq + lax.broadcasted_iota(jnp.int32, (1, tq), 1)
    blk = lax.broadcasted_iota(jnp.int32, (n_sel, 1), 0)
    cur = pos_t // SEL_BLOCK
    forced = (blk == 0) | (blk == cur) | (blk == cur - 1)
    score = jnp.where(blk * SEL_BLOCK <= pos_t, imp_t + jnp.where(forced, FORCE_BONUS, 0.0), NEG_INF)
    sel_t = _rank_select(score, n_sel).astype(BF16)

    blk_of_key = lax.broadcasted_iota(jnp.int32, (n_sel, tq), 1) // SEL_BLOCK
    blk_row = lax.broadcasted_iota(jnp.int32, (n_sel, tq), 0)
    key_off = lax.broadcasted_iota(jnp.int32, (1, tq), 1)

    def sel_body(kt, carry):
        m, l, acc = carry
        start = pl.multiple_of(kt * tq, tq)
        kd = ks_ref[pl.ds(start, tq), :]
        vd = vs_ref[pl.ds(start, tq), :]
        expand = jnp.where(blk_of_key + kt * SEL_PER_TILE == blk_row, 1.0, 0.0).astype(BF16)
        mask = (_dot_tn(sel_t, expand) > 0.5) & (kt * tq + key_off <= pos)
        return _flash_tile(qm, kd, jnp.where(lower, vd, zero), jnp.where(lower, zero, vd), mask, m, l, acc)

    init = ([jnp.full((tq, 1), NEG_INF, F32)] * GROUP, [jnp.zeros((tq, 1), F32)] * GROUP,
            [jnp.zeros((tq, LANES), F32)] * (GROUP // 2))
    m, l, acc = lax.fori_loop(0, qi + 1, sel_body, init)
    o_sel = [acc[j] / jnp.where(lower, l[2 * j], l[2 * j + 1]) for j in range(GROUP // 2)]

    m, l, acc = init
    for d, (kw_ref, vw_ref) in enumerate(((kw0_ref, vw0_ref), (kw1_ref, vw1_ref), (kw2_ref, vw2_ref))):
        kpos = (qi - d) * tq + key_off
        diff = pos - kpos
        mask = (diff >= 0) & (diff < WINDOW) & (kpos >= 0)
        vd = vw_ref[...]
        m, l, acc = _flash_tile(qm, kw_ref[...], jnp.where(lower, vd, zero), jnp.where(lower, zero, vd), mask, m, l, acc)
    o_win = [acc[j] / jnp.where(lower, l[2 * j], l[2 * j + 1]) for j in range(GROUP // 2)]

    for j in range(GROUP // 2):
        sl = slice(j * LANES, (j + 1) * LANES)
        o = bg0_ref[:, sl] * o_cmp[j] + bg1_ref[:, sl] * o_sel[j] + bg2_ref[:, sl] * o_win[j]
        o_ref[:, sl] = o.astype(BF16)


def _nsa_prompt(q, kc, vc, kdup_s, vdup_s, kdup_w, vdup_w, bgx, n, t):
    tq = NSA_TQ
    nq = t // tq
    nc = t // CMP_BLOCK
    hw = GROUP * HEAD_DIM
    qspec = pl.BlockSpec((tq, hw), lambda i, j, h: (i * nq + j, h))
    cspec = pl.BlockSpec((nc, LANES), lambda i, j, h: (i, h // 2))
    seq_spec = pl.BlockSpec((t, LANES), lambda i, j, h: (i, h))
    win_specs = [pl.BlockSpec((tq, LANES), functools.partial(lambda d, i, j, h: (i * nq + jnp.maximum(j - d, 0), h), d))
                 for d in range(WIN_TILES)]
    bg_specs = [pl.BlockSpec((tq, hw), functools.partial(lambda b, i, j, h: (i * nq + j, b * N_KV_HEADS + h), b))
                for b in range(3)]
    return pl.pallas_call(
        _nsa_prompt_kernel,
        grid=(n, nq, N_KV_HEADS),
        in_specs=[qspec, cspec, cspec, seq_spec, seq_spec] + win_specs + win_specs + bg_specs,
        out_specs=qspec,
        out_shape=jax.ShapeDtypeStruct((n * t, C_Q), BF16),
        compiler_params=_params(("parallel", "parallel", "arbitrary")),
        name="nsa_prompt",
    )(q, kc, vc, kdup_s, vdup_s, kdup_w, kdup_w, kdup_w, vdup_w, vdup_w, vdup_w, bgx, bgx, bgx)


STEP_COLS = N_KV_HEADS * 4 * GROUP
STEP_KT = 512
NEW_ROWS = 8
LOWEST = -3.0e38


def _topk_rows(score, k):
    n = score.shape[0]
    row = lax.broadcasted_iota(jnp.int32, (n, 1), 0)
    sel = jnp.zeros(score.shape, F32)
    for _ in range(k):
        mx = jnp.max(score, axis=0, keepdims=True)
        first = jnp.min(jnp.where(score == mx, row, n), axis=0, keepdims=True)
        pick = row == first
        sel = jnp.where(pick, 1.0, sel)
        score = jnp.where(pick, LOWEST, score)
    return sel


def _sel_page_copy(pool_ref, pt_ref, buf_ref, sem, seq, i):
    return pltpu.make_async_copy(pool_ref.at[pt_ref[seq, i]], buf_ref.at[pl.ds(i * PAGE_SIZE, PAGE_SIZE), :], sem)


def _nsa_step_kernel(pt_ref, qbd_ref, gate_ref, kc_ref, vc_ref, snew_ref, wbuf_ref, wnew_ref, pool_ref,
                     x_ref, wout_ref, buf_ref, s_ref, sel_ref, sem):
    seq = pl.program_id(0)
    n_pages = pt_ref.shape[1]
    past_len = n_pages * PAGE_SIZE
    for i in range(n_pages):
        _sel_page_copy(pool_ref, pt_ref, buf_ref, sem, seq, i).start()

    half = KV_ROW // 2
    qbd = qbd_ref[0]
    col = lax.broadcasted_iota(jnp.int32, (1, STEP_COLS), 1)
    t_col = (col // GROUP) % 4
    qpos = past_len + t_col
    ones_v = jnp.ones((STEP_KT, LANES), BF16)

    kc = kc_ref[...]
    nc = kc.shape[0]
    s = _dot(kc.astype(BF16), qbd)
    cmask = (lax.broadcasted_iota(jnp.int32, (nc, 1), 0) + 1) * CMP_BLOCK - 1 <= qpos
    s = jnp.where(cmask, s, NEG_INF)
    e = jnp.where(cmask, jnp.exp(s - jnp.max(s, axis=0, keepdims=True)), 0.0)
    pr = e / jnp.maximum(jnp.sum(e, axis=0, keepdims=True), TINY)
    o_cmp = _dot_tn(pr.astype(BF16), vc_ref[...].astype(BF16))

    n_rows = sel_ref.shape[0]
    pair = jnp.where(lax.broadcasted_iota(jnp.int32, (n_rows, nc), 1) // CMP_PER_SEL
                     == lax.broadcasted_iota(jnp.int32, (n_rows, nc), 0), 1.0, 0.0).astype(BF16)
    group = jnp.where(lax.broadcasted_iota(jnp.int32, (STEP_COLS, STEP_COLS), 0) // GROUP
                      == lax.broadcasted_iota(jnp.int32, (STEP_COLS, STEP_COLS), 1) // GROUP, 1.0, 0.0).astype(BF16)
    pr_hi, pr_lo = _split_bf16(pr)
    a_hi, a_lo = _split_bf16(_dot(pair, pr_hi) + _dot(pair, pr_lo))
    imp = _dot(a_hi, group) + _dot(a_lo, group)
    blk = lax.broadcasted_iota(jnp.int32, (n_rows, 1), 0)
    cur = qpos // SEL_BLOCK
    forced = (blk == 0) | (blk == cur) | (blk == cur - 1)
    score = jnp.where(blk * SEL_BLOCK <= qpos, imp + jnp.where(forced, FORCE_BONUS, 0.0), NEG_INF)
    sel_ref[...] = _topk_rows(score, SEL_TOPK)

    for i in range(n_pages):
        _sel_page_copy(pool_ref, pt_ref, buf_ref, sem, seq, i).wait()
    blocks_per_tile = STEP_KT // SEL_BLOCK
    expand = jnp.where(lax.broadcasted_iota(jnp.int32, (STEP_KT, blocks_per_tile), 0) // SEL_BLOCK
                       == lax.broadcasted_iota(jnp.int32, (STEP_KT, blocks_per_tile), 1), 1.0, 0.0).astype(BF16)
    n_tiles = past_len // STEP_KT

    def score_tile(kt, m):
        start = pl.multiple_of(kt * STEP_KT, STEP_KT)
        k = buf_ref[pl.ds(start, STEP_KT), 0:half].astype(BF16)
        sel_blocks = sel_ref[pl.ds(pl.multiple_of(kt * blocks_per_tile, blocks_per_tile), blocks_per_tile), :]
        mask = _dot(expand, sel_blocks.astype(BF16)) > 0.5
        st = jnp.where(mask, _dot(k, qbd), NEG_INF)
        s_ref[pl.ds(start, STEP_KT), :] = st
        return jnp.maximum(m, jnp.max(st, axis=0, keepdims=True))

    m = lax.fori_loop(0, n_tiles, score_tile, jnp.full((1, STEP_COLS), NEG_INF, F32))
    new_row = lax.broadcasted_iota(jnp.int32, (NEW_ROWS, 1), 0)
    new_vis = new_row <= t_col
    n_past_blk = past_len // SEL_BLOCK
    snew = snew_ref[0]
    s_new = jnp.where(new_vis & (sel_ref[n_past_blk:n_past_blk + 1, :] > 0.5), _dot(snew[:, 0:half].astype(BF16), qbd), NEG_INF)
    m = jnp.maximum(m, jnp.max(s_new, axis=0, keepdims=True))

    def value_tile(kt, carry):
        acc, l = carry
        start = pl.multiple_of(kt * STEP_KT, STEP_KT)
        p = jnp.exp(s_ref[pl.ds(start, STEP_KT), :] - m).astype(BF16)
        v = buf_ref[pl.ds(start, STEP_KT), half:KV_ROW].astype(BF16)
        return acc + _dot_tn(p, v), l + _dot_tn(p, ones_v)

    acc, l = lax.fori_loop(0, n_tiles, value_tile, (jnp.zeros((STEP_COLS, half), F32), jnp.zeros((STEP_COLS, LANES), F32)))
    p_new = jnp.exp(s_new - m).astype(BF16)
    acc = acc + _dot_tn(p_new, snew[:, half:KV_ROW].astype(BF16))
    l = l + _dot_tn(p_new, ones_v[0:NEW_ROWS])
    o_sel = acc / jnp.concatenate([l, l], axis=1)

    wbuf = wbuf_ref[0]
    lb = wbuf.shape[0]
    wnew = wnew_ref[0]
    kpos = past_len - lb + lax.broadcasted_iota(jnp.int32, (lb, 1), 0)
    diff = qpos - kpos
    s_buf = jnp.where((diff >= 0) & (diff < WINDOW) & (kpos >= 0), _dot(wbuf[:, 0:half].astype(BF16), qbd), NEG_INF)
    s_wn = jnp.where(new_vis, _dot(wnew[:, 0:half].astype(BF16), qbd), NEG_INF)
    mw = jnp.maximum(jnp.max(s_buf, axis=0, keepdims=True), jnp.max(s_wn, axis=0, keepdims=True))
    p_buf = jnp.exp(s_buf - mw).astype(BF16)
    p_wn = jnp.exp(s_wn - mw).astype(BF16)
    acc_w = _dot_tn(p_buf, wbuf[:, half:KV_ROW].astype(BF16)) + _dot_tn(p_wn, wnew[:, half:KV_ROW].astype(BF16))
    l_w = _dot_tn(p_buf, ones_v[0:lb]) + _dot_tn(p_wn, ones_v[0:NEW_ROWS])
    o_win = acc_w / jnp.concatenate([l_w, l_w], axis=1)

    gate = gate_ref[0]
    x_ref[0] = gate[:, 0:1] * o_cmp + gate[:, 1:2] * o_sel + gate[:, 2:3] * o_win

    n_new = 4
    wout_ref[0, 0:lb - n_new, :] = wbuf[n_new:lb, :]
    wout_ref[0, lb - n_new:lb, :] = wnew[0:n_new, :]


def _nsa_step(qbd, gates, kc, vc, snew8, wbuf, wnew8, pool, page_table):
    n_seq, n_pages = page_table.shape
    past_len = n_pages * PAGE_SIZE
    nc = past_len // CMP_BLOCK
    lb = wbuf.shape[1]
    n_rows = -(-(past_len // SEL_BLOCK + 1) // 8) * 8
    pool2 = pool.reshape(pool.shape[0], PAGE_SIZE, KV_ROW)
    per_seq = lambda shape: pl.BlockSpec((1,) + shape, lambda i, pt: (i, 0, 0))
    grid_spec = pltpu.PrefetchScalarGridSpec(
        num_scalar_prefetch=1,
        grid=(n_seq,),
        in_specs=[per_seq((N_KV_HEADS * HEAD_DIM, STEP_COLS)), per_seq((STEP_COLS, LANES)),
                  pl.BlockSpec((nc, KV_ROW // 2), lambda i, pt: (i, 0)), pl.BlockSpec((nc, KV_ROW // 2), lambda i, pt: (i, 0)),
                  per_seq((NEW_ROWS, KV_ROW)), per_seq((lb, KV_ROW)), per_seq((NEW_ROWS, KV_ROW)),
                  pl.BlockSpec(memory_space=pl.ANY)],
        out_specs=[per_seq((STEP_COLS, KV_ROW // 2)), per_seq((lb, KV_ROW))],
        scratch_shapes=[pltpu.VMEM((past_len, KV_ROW), F32), pltpu.VMEM((past_len, STEP_COLS), F32),
                        pltpu.VMEM((n_rows, STEP_COLS), F32), pltpu.SemaphoreType.DMA(())],
    )
    return pl.pallas_call(
        _nsa_step_kernel,
        grid_spec=grid_spec,
        out_shape=[jax.ShapeDtypeStruct((n_seq, STEP_COLS, KV_ROW // 2), F32),
                   jax.ShapeDtypeStruct((n_seq, lb, KV_ROW), F32)],
        compiler_params=_params(("arbitrary",)),
        name="nsa_step",
    )(page_table, qbd, gates, kc, vc, snew8, wbuf, wnew8, pool2)


def _step_query_layout(q_s, n_seq, t):
    q5 = q_s.reshape(n_seq, t, N_KV_HEADS, GROUP, HEAD_DIM)
    qt = jnp.transpose(q5, (0, 2, 4, 1, 3)).reshape(n_seq, N_KV_HEADS, HEAD_DIM, t * GROUP)
    eye = jnp.eye(N_KV_HEADS, dtype=q_s.dtype)
    return jnp.einsum('nhdc,hk->nhdkc', qt, eye).reshape(n_seq, N_KV_HEADS * HEAD_DIM, N_KV_HEADS * t * GROUP)


def _step_gate_layout(bgx_s, n_seq, t):
    bg = bgx_s.reshape(n_seq, t, 3, N_KV_HEADS, GROUP, HEAD_DIM)[..., 0]
    bg = jnp.transpose(bg, (0, 3, 1, 4, 2)).reshape(n_seq, N_KV_HEADS * t * GROUP, 3)
    return jnp.pad(bg, ((0, 0), (0, 0), (0, LANES - 3)))


def _step_output_layout(x, n_seq, t):
    x6 = x.reshape(n_seq, N_KV_HEADS, t, GROUP, N_KV_HEADS, HEAD_DIM)
    diag = jnp.stack([x6[:, h, :, :, h, :] for h in range(N_KV_HEADS)], axis=2)
    return diag.reshape(n_seq * t, C_Q)


PEER_HALF = PEER_KEY_DIM // 2
N_SIDES = 2 * PEER_HEADS


def _merge_kernel(x_ref, conv_ref, o_ref, gmix_ref, wmg_ref, wao_ref, wo_ref, gffn_ref, wq_ref, sk_ref,
                  h_ref, hn_ref, st_ref):
    x = x_ref[...]
    n = (x * lax.rsqrt(jnp.mean(x * x, axis=-1, keepdims=True) + NORM_EPS) * gmix_ref[...]).astype(BF16)
    mg = _sigmoid(_dot(n, wmg_ref[...]))
    attn = _dot(o_ref[...], wao_ref[...])
    mix = mg[:, :D_MODEL] * conv_ref[...] + mg[:, D_MODEL:] * attn
    h = x + _dot(mix.astype(BF16), wo_ref[...])
    h_ref[...] = h
    hn = (h * lax.rsqrt(jnp.mean(h * h, axis=-1, keepdims=True) + NORM_EPS) * gffn_ref[...]).astype(BF16)
    hn_ref[...] = hn
    qp = _dot(hn, wq_ref[...]).astype(BF16)
    for i in range(N_SIDES):
        st_ref[i * N_KEYS:(i + 1) * N_KEYS, :] = _dot_nt(sk_ref[i], qp[:, i * PEER_HALF:(i + 1) * PEER_HALF])


def _merge(x2d, conv_out, o, w, row_tile):
    ntok = x2d.shape[0]
    row = lambda width: pl.BlockSpec((row_tile, width), lambda i: (i, 0))
    consts = (w['g_mix'], w['w_mg'], w['w_attn_out'], w['w_o'], w['g_ffn'], w['peer_wq'], w['peer_subkeys'])
    return pl.pallas_call(
        _merge_kernel,
        grid=(ntok // row_tile,),
        in_specs=[row(D_MODEL), row(D_MODEL), row(C_Q)] + [_const_spec(c.shape) for c in consts],
        out_specs=[row(D_MODEL), row(D_MODEL), pl.BlockSpec((N_SIDES * N_KEYS, row_tile), lambda i: (0, i))],
        out_shape=[jax.ShapeDtypeStruct((ntok, D_MODEL), F32), jax.ShapeDtypeStruct((ntok, D_MODEL), BF16),
                   jax.ShapeDtypeStruct((N_SIDES * N_KEYS, ntok), F32)],
        compiler_params=_params(("parallel",)),
        name="merge",
    )(x2d, conv_out, o, *consts)


def _cand_groups():
    groups = []
    for j in range(PEER_TOPK):
        n_valid = PEER_TOPK // (j + 1)
        groups.append((j, n_valid, -(-n_valid // 8) * 8))
    return groups


def _top_rows_sorted(s, k):
    n = s.shape[0]
    row = lax.broadcasted_iota(jnp.int32, (n, 1), 0).astype(F32)
    out_row = lax.broadcasted_iota(jnp.int32, (k, 1), 0)
    vals = jnp.zeros((k, s.shape[1]), F32)
    idxs = jnp.zeros((k, s.shape[1]), F32)
    for r in range(k):
        mx = jnp.max(s, axis=0, keepdims=True)
        first = jnp.min(jnp.where(s == mx, row, float(n)), axis=0, keepdims=True)
        vals = jnp.where(out_row == r, mx, vals)
        idxs = jnp.where(out_row == r, first, idxs)
        s = jnp.where(row == first, LOWEST, s)
    return vals, idxs


def _peer_topk_kernel(st_ref, a_ref, b_ref, g_ref):
    tt = st_ref.shape[1]
    groups = _cand_groups()
    row16 = lax.broadcasted_iota(jnp.int32, (PEER_TOPK, 1), 0).astype(F32)
    a_rows, b_rows, g_rows = [], [], []
    for hd in range(PEER_HEADS):
        va, ia = _top_rows_sorted(st_ref[(2 * hd) * N_KEYS:(2 * hd + 1) * N_KEYS, :], PEER_TOPK)
        vb, ib = _top_rows_sorted(st_ref[(2 * hd + 1) * N_KEYS:(2 * hd + 2) * N_KEYS, :], PEER_TOPK)
        cands, flats = [], []
        for j, n_valid, n_rows in groups:
            i_idx = lax.broadcasted_iota(jnp.int32, (n_rows, 1), 0)
            cands.append(jnp.where(i_idx < n_valid, va[0:n_rows, :] + vb[j:j + 1, :], LOWEST))
            flats.append((i_idx * PEER_TOPK + j).astype(F32))
        cand = jnp.concatenate(cands, axis=0)
        flat = jnp.concatenate(flats, axis=0)
        n_flat = float(PEER_TOPK * PEER_TOPK)
        sc = jnp.zeros((PEER_TOPK, tt), F32)
        ea = jnp.zeros((PEER_TOPK, tt), F32)
        eb = jnp.zeros((PEER_TOPK, tt), F32)
        for r in range(PEER_TOPK):
            mx = jnp.max(cand, axis=0, keepdims=True)
            first = jnp.min(jnp.where(cand == mx, flat, n_flat), axis=0, keepdims=True)
            cand = jnp.where(flat == first, LOWEST, cand)
            i_sel = jnp.floor(first * (1.0 / PEER_TOPK))
            j_sel = first - i_sel * PEER_TOPK
            a_sel = jnp.sum(jnp.where(row16 == i_sel, ia, 0.0), axis=0, keepdims=True)
            b_sel = jnp.sum(jnp.where(row16 == j_sel, ib, 0.0), axis=0, keepdims=True)
            sc = jnp.where(row16 == r, mx, sc)
            ea = jnp.where(row16 == r, a_sel, ea)
            eb = jnp.where(row16 == r, b_sel, eb)
        e = jnp.exp(sc - sc[0:1, :])
        g_rows.append(e / jnp.sum(e, axis=0, keepdims=True))
        a_rows.append(ea)
        b_rows.append(eb)
    a_ref[...] = jnp.transpose(jnp.concatenate(a_rows, axis=0))
    b_ref[...] = jnp.transpose(jnp.concatenate(b_rows, axis=0))
    g_ref[...] = jnp.transpose(jnp.concatenate(g_rows, axis=0))


def _peer_topk(scores_t, tok_tile):
    ntok = scores_t.shape[1]
    n_pick = PEER_HEADS * PEER_TOPK
    out = pl.BlockSpec((tok_tile, n_pick), lambda i: (i, 0))
    return pl.pallas_call(
        _peer_topk_kernel,
        grid=(ntok // tok_tile,),
        in_specs=[pl.BlockSpec((N_SIDES * N_KEYS, tok_tile), lambda i: (0, i))],
        out_specs=[out, out, out],
        out_shape=[jax.ShapeDtypeStruct((ntok, n_pick), F32)] * 3,
        compiler_params=_params(("parallel",)),
        name="peer_topk",
    )(scores_t)


W_TOK = 64


def _w_row_copy(s_ref, w_hbm, sem, step, a):
    return pltpu.make_async_copy(s_ref.at[:, a, :], w_hbm.at[a, pl.ds(step * W_TOK, W_TOK), :], sem)


def _peer_w_kernel(a_ref, b_ref, g_ref, w_hbm, s_ref, sem):
    step = pl.program_id(0)
    sub = lax.broadcasted_iota(jnp.int32, (N_KEYS, 1), 0).astype(F32)

    def body(t, carry):
        a_row = a_ref[pl.ds(t, 1), :]
        b_row = b_ref[pl.ds(t, 1), :]
        g_row = g_ref[pl.ds(t, 1), :]
        ga = jnp.where(sub == a_row, g_row, 0.0).astype(BF16)
        ob = jnp.where(sub == b_row, 1.0, 0.0).astype(BF16)
        s_ref[t] = _dot_nt(ga, ob)
        return carry

    lax.fori_loop(0, W_TOK, body, 0)
    for a in range(N_KEYS):
        _w_row_copy(s_ref, w_hbm, sem, step, a).start()
    for a in range(N_KEYS):
        _w_row_copy(s_ref, w_hbm, sem, step, a).wait()


def _peer_weights(a_idx, b_idx, gate):
    ntok, n_pick = a_idx.shape
    row = pl.BlockSpec((W_TOK, n_pick), lambda i: (i, 0))
    return pl.pallas_call(
        _peer_w_kernel,
        grid=(ntok // W_TOK,),
        in_specs=[row, row, row],
        out_specs=pl.BlockSpec(memory_space=pl.ANY),
        out_shape=jax.ShapeDtypeStruct((N_KEYS, ntok, N_KEYS), F32),
        scratch_shapes=[pltpu.VMEM((W_TOK, N_KEYS, N_KEYS), F32), pltpu.SemaphoreType.DMA(())],
        compiler_params=_params(("arbitrary",)),
        name="peer_weights",
    )(a_idx, b_idx, gate)


PEER_TOK = 512
PEER_ABLK = 16


def _peer_dense_kernel(hn_ref, h_ref, w_ref, u_ref, v_ref, y_ref, acc_ref):
    j = pl.program_id(1)

    @pl.when(j == 0)
    def _():
        acc_ref[...] = jnp.zeros(acc_ref.shape, F32)

    hn = hn_ref[...]
    pair = 2 * N_KEYS
    for k in range(PEER_ABLK // 2):
        act = _dot_nt(hn, u_ref[k * pair:(k + 1) * pair, :])
        wk = jnp.concatenate([w_ref[2 * k], w_ref[2 * k + 1]], axis=1)
        acc_ref[...] += _dot((wk * _gelu(act)).astype(BF16), v_ref[k * pair:(k + 1) * pair, :])

    @pl.when(j == pl.num_programs(1) - 1)
    def _():
        y_ref[...] = h_ref[...] + acc_ref[...]


def _peer_dense(hn, h, w_atb, u, v):
    ntok = hn.shape[0]
    blk = PEER_ABLK * N_KEYS
    tok = lambda: pl.BlockSpec((PEER_TOK, D_MODEL), lambda i, j: (i, 0))
    exp = lambda: pl.BlockSpec((blk, D_MODEL), lambda i, j: (j, 0))
    return pl.pallas_call(
        _peer_dense_kernel,
        grid=(ntok // PEER_TOK, N_KEYS // PEER_ABLK),
        in_specs=[tok(), tok(), pl.BlockSpec((PEER_ABLK, PEER_TOK, N_KEYS), lambda i, j: (j, i, 0)), exp(), exp()],
        out_specs=tok(),
        out_shape=jax.ShapeDtypeStruct((ntok, D_MODEL), F32),
        scratch_shapes=[pltpu.VMEM((PEER_TOK, D_MODEL), F32)],
        compiler_params=_params(("parallel", "arbitrary")),
        name="peer_dense",
    )(hn, h, w_atb, u, v)


def kernel(x_prompt, x_sample, cache_cmp_kv, cache_sel_kv, state_win_kv, state_conv, page_table, g_mix, w_in, conv_w, conv_b, conv_ln_g, conv_ln_b, w_conv_out, q_gain, k_gain, cmp_pe, cmp_w1, cmp_w2, w_attn_out, w_o, g_ffn, peer_wq, peer_subkeys, peer_u, peer_v):
    layer = 0
    p = {'g_mix': g_mix, 'w_in': w_in, 'conv_w': conv_w, 'conv_b': conv_b, 'conv_ln_g': conv_ln_g,
         'conv_ln_b': conv_ln_b, 'w_conv_out': w_conv_out, 'q_gain': q_gain, 'k_gain': k_gain, 'cmp_pe': cmp_pe,
         'cmp_w1': cmp_w1, 'cmp_w2': cmp_w2, 'w_attn_out': w_attn_out, 'w_o': w_o, 'g_ffn': g_ffn,
         'peer_wq': peer_wq, 'peer_subkeys': peer_subkeys}
    w = _prep_weights({k: v[layer] for k, v in p.items()})
    u_bf = peer_u[layer].astype(BF16)
    v_bf = peer_v[layer].astype(BF16)

    def ffn(x2d, conv_out, o):
        h, hn, scores_t = _merge(x2d, conv_out, o, w, ROW_TILE)
        a_idx, b_idx, gate = _peer_topk(scores_t, ROW_TILE)
        return _peer_dense(hn, h, _peer_weights(a_idx, b_idx, gate), u_bf, v_bf)

    n, t, _ = x_prompt.shape
    xp = x_prompt.reshape(n * t, D_MODEL)
    cos_p, sin_p = _rope_tables(jnp.arange(t, dtype=jnp.int32))
    glu, q, ckv, skv, wkv, bgx, kds, vds, kdw, vdw = _project(xp, cos_p, sin_p, w, ROW_TILE)
    glu3 = glu.reshape(n, t, D_CONV)
    conv_p = _conv_prompt(glu3, w, ROW_TILE).reshape(n * t, D_MODEL)
    kc, vc = _compress_prompt(ckv, w, 128)
    o_p = _nsa_prompt(q, kc, vc, kds, vds, kdw, vdw, bgx, n, t)
    y_p = ffn(xp, conv_p, o_p).reshape(n, t, D_MODEL)

    n_s, t_s, _ = x_sample.shape
    past_len = page_table.shape[1] * PAGE_SIZE
    xs = x_sample.reshape(n_s * t_s, D_MODEL)
    pos_s = past_len + jnp.arange(t_s, dtype=jnp.int32)
    cos_s, sin_s = _rope_tables(jnp.tile(pos_s, ROW_TILE // t_s))
    glu_s, q_s, ckv_s, skv_s, wkv_s, bgx_s = _project(xs, cos_s, sin_s, w, ROW_TILE)[:6]
    glu_s3 = glu_s.reshape(n_s, t_s, D_CONV)
    conv_s = _conv_step(jnp.swapaxes(state_conv[layer], 0, 1), jnp.swapaxes(glu_s3, 0, 1), w)
    conv_s = jnp.swapaxes(conv_s, 0, 1).reshape(n_s * t_s, D_MODEL)
    kc_s, vc_s = _compress_pages(cache_cmp_kv[layer], page_table, w)
    lb = state_win_kv.shape[2]
    pad_new = lambda a: jnp.pad(a.reshape(n_s, t_s, KV_ROW), ((0, 0), (0, NEW_ROWS - t_s), (0, 0)))
    x_step, win_s = _nsa_step(_step_query_layout(q_s, n_s, t_s), _step_gate_layout(bgx_s, n_s, t_s), kc_s, vc_s,
                              pad_new(skv_s), state_win_kv[layer].reshape(n_s, lb, KV_ROW), pad_new(wkv_s),
                              cache_sel_kv[layer], page_table)
    o_s = _step_output_layout(x_step, n_s, t_s).astype(BF16)
    y_s = ffn(xs, conv_s, o_s).reshape(n_s, t_s, D_MODEL)

    kv6 = lambda a, nn, tt: a.reshape(1, nn, tt, 2, N_KV_HEADS, HEAD_DIM)
    n_win = min(WINDOW, t)
    new_conv_s = jnp.concatenate([state_conv[layer][:, t_s:], glu_s3], axis=1)
    return (y_p, y_s,
            kv6(ckv, n, t), kv6(skv, n, t), kv6(wkv, n, t)[:, :, t - n_win:], glu3[None, :, t - (CONV_WIDTH - 1):],
            kv6(ckv_s, n_s, t_s), kv6(skv_s, n_s, t_s), kv6(win_s, n_s, lb), new_conv_s[None])
```

```python
import functools
import math

import jax
import jax.numpy as jnp
from jax import lax
from jax.experimental import pallas as pl
from jax.experimental.pallas import tpu as pltpu

F32 = jnp.float32
BF16 = jnp.bfloat16

D_MODEL = 1024
D_CONV = 512
CONV_WIDTH = 31
N_HEADS = 16
N_KV_HEADS = 4
HEAD_DIM = 64
GROUP = N_HEADS // N_KV_HEADS
CMP_BLOCK = 32
CMP_HIDDEN = 128
SEL_BLOCK = 64
SEL_TOPK = 16
WINDOW = 512
ROPE_THETA = 10000.0
FORCE_BONUS = 1.0e3
N_KEYS = 128
PEER_HEADS = 8
PEER_TOPK = 16
PEER_KEY_DIM = 256
NORM_EPS = 1e-6
NEG_INF = -1.0e30
TINY = 1.0e-30
ATTN_SCALE = HEAD_DIM ** -0.5
KV_ROW = 2 * N_KV_HEADS * HEAD_DIM
C_Q = N_HEADS * HEAD_DIM

ROW_TILE = 256
LANES = 128
VMEM_LIMIT = 56 * 1024 * 1024


def _dot(a, b):
    return jnp.dot(a, b, preferred_element_type=F32)


def _dot_nt(a, b):
    return lax.dot_general(a, b, (((1,), (1,)), ((), ())), preferred_element_type=F32)


def _dot_tn(a, b):
    return lax.dot_general(a, b, (((0,), (0,)), ((), ())), preferred_element_type=F32)


def _split_bf16(x):
    hi = x.astype(BF16)
    lo = (x - hi.astype(F32)).astype(BF16)
    return hi, lo


def _sigmoid(x):
    return 1.0 / (1.0 + jnp.exp(-x))


def _params(sem):
    return pltpu.CompilerParams(dimension_semantics=sem, vmem_limit_bytes=VMEM_LIMIT)


def _const_spec(shape):
    nd = len(shape)
    return pl.BlockSpec(shape, lambda *_: (0,) * nd)


def _head_pair_ones():
    r = lax.broadcasted_iota(jnp.int32, (LANES, LANES), 0) // HEAD_DIM
    c = lax.broadcasted_iota(jnp.int32, (LANES, LANES), 1) // HEAD_DIM
    return jnp.where(r == c, 1.0, 0.0).astype(BF16)


def _head_rms_rope(z, gain, cos, sin_signed, ones_bd, first_half):
    hi, lo = _split_bf16(z * z)
    ss = _dot(hi, ones_bd) + _dot(lo, ones_bd)
    zn = z * lax.rsqrt(ss * (1.0 / HEAD_DIM) + NORM_EPS) * gain
    partner = jnp.where(first_half, pltpu.roll(zn, LANES - HEAD_DIM // 2, 1), pltpu.roll(zn, HEAD_DIM // 2, 1))
    return zn * cos + partner * sin_signed


def _proj_kernel(x_ref, cos_ref, sin_ref, gmix_ref, wglu_ref, wq_ref, wkv_ref, wbg_ref, qg_ref, kg_ref,
                 glu_ref, q_ref, ckv_ref, skv_ref, wkvo_ref, bg_ref, *dup_refs):
    x = x_ref[...]
    ms = jnp.mean(x * x, axis=-1, keepdims=True)
    n = (x * lax.rsqrt(ms + NORM_EPS) * gmix_ref[...]).astype(BF16)
    cos = cos_ref[...]
    sin_signed = sin_ref[...]
    ones_bd = _head_pair_ones()
    lane = lax.broadcasted_iota(jnp.int32, (1, LANES), 1)
    first_half = (lane % HEAD_DIM) < (HEAD_DIM // 2)

    zg = _dot(n, wglu_ref[...])
    glu_ref[...] = zg[:, :D_CONV] * _sigmoid(zg[:, D_CONV:])

    zq = _dot(n, wq_ref[...])
    qg = qg_ref[...]
    for j in range(C_Q // LANES):
        sl = slice(j * LANES, (j + 1) * LANES)
        qj = _head_rms_rope(zq[:, sl], qg, cos, sin_signed, ones_bd, first_half)
        q_ref[:, sl] = (qj * ATTN_SCALE).astype(BF16)

    zkv = _dot(n, wkv_ref[...])
    half = KV_ROW // 2
    lower = lane < HEAD_DIM
    for b, out_ref in enumerate((ckv_ref, skv_ref, wkvo_ref)):
        kg = kg_ref[b:b + 1, :]
        for j in range(half // LANES):
            sl = slice(b * KV_ROW + j * LANES, b * KV_ROW + (j + 1) * LANES)
            kj = _head_rms_rope(zkv[:, sl], kg, cos, sin_signed, ones_bd, first_half)
            out_ref[:, j * LANES:(j + 1) * LANES] = kj
            vj = zkv[:, b * KV_ROW + half + j * LANES:b * KV_ROW + half + (j + 1) * LANES]
            if b > 0:
                for src, dup_ref in ((kj, dup_refs[2 * (b - 1)]), (vj, dup_refs[2 * (b - 1) + 1])):
                    swapped = pltpu.roll(src, HEAD_DIM, 1)
                    dup_ref[:, 2 * j * LANES:(2 * j + 1) * LANES] = jnp.where(lower, src, swapped).astype(BF16)
                    dup_ref[:, (2 * j + 1) * LANES:(2 * j + 2) * LANES] = jnp.where(lower, swapped, src).astype(BF16)
        out_ref[:, half:] = zkv[:, b * KV_ROW + half:(b + 1) * KV_ROW]

    bg_ref[...] = _sigmoid(_dot(n, wbg_ref[...]))


def _rope_tables(pos):
    half = HEAD_DIM // 2
    inv = jnp.exp(-(2.0 * math.log(ROPE_THETA) / HEAD_DIM) * jnp.arange(half, dtype=F32))
    ang = pos.astype(F32)[:, None] * inv[None, :]
    lane = jnp.arange(LANES)
    cos = jnp.cos(ang)[:, lane % half]
    sin = jnp.sin(ang)[:, lane % half]
    sin_signed = jnp.where((lane % HEAD_DIM) < half, -sin, sin)
    return cos, sin_signed


def _project(x2d, cos, sin_signed, w, row_tile):
    ntok = x2d.shape[0]
    n_pos_blocks = cos.shape[0] // row_tile
    row = lambda width: pl.BlockSpec((row_tile, width), lambda i: (i, 0))
    tab = pl.BlockSpec((row_tile, LANES), lambda i: (i % n_pos_blocks, 0))
    consts = (w['g_mix'], w['w_glu'], w['w_q'], w['w_kv'], w['w_bgx'], w['q_gain2'], w['k_gain2'])
    return pl.pallas_call(
        _proj_kernel,
        grid=(ntok // row_tile,),
        in_specs=[row(D_MODEL), tab, tab] + [_const_spec(c.shape) for c in consts],
        out_specs=[row(D_CONV), row(C_Q), row(KV_ROW), row(KV_ROW), row(KV_ROW), row(3 * C_Q)] + [row(KV_ROW)] * 4,
        out_shape=[jax.ShapeDtypeStruct((ntok, D_CONV), F32), jax.ShapeDtypeStruct((ntok, C_Q), BF16),
                   jax.ShapeDtypeStruct((ntok, KV_ROW), F32), jax.ShapeDtypeStruct((ntok, KV_ROW), F32),
                   jax.ShapeDtypeStruct((ntok, KV_ROW), F32), jax.ShapeDtypeStruct((ntok, 3 * C_Q), F32)]
        + [jax.ShapeDtypeStruct((ntok, KV_ROW), BF16)] * 4,
        compiler_params=_params(("parallel",)),
        name="proj",
    )(x2d, cos, sin_signed, *consts)


def _prep_weights(p):
    w_in = p['w_in']
    c_glu = 2 * D_CONV
    c_kv = 3 * KV_ROW
    c_bg = 3 * N_HEADS
    o1, o2, o3 = c_glu, c_glu + C_Q, c_glu + C_Q + c_kv
    o4 = o3 + c_bg
    row = lambda v: v.reshape(1, -1)
    return {
        'g_mix': row(p['g_mix']),
        'w_glu': w_in[:, :o1].astype(BF16),
        'w_q': w_in[:, o1:o2].astype(BF16),
        'w_kv': w_in[:, o2:o3].astype(BF16),
        'w_bgx': jnp.repeat(w_in[:, o3:o4], HEAD_DIM, axis=1).astype(BF16),
        'w_mg': w_in[:, o4:].astype(BF16),
        'q_gain2': row(jnp.tile(p['q_gain'], 2)),
        'k_gain2': jnp.tile(p['k_gain'], (1, 2)),
        'conv_w': p['conv_w'],
        'conv_b': row(p['conv_b']),
        'conv_ln_g': row(p['conv_ln_g']),
        'conv_ln_b': row(p['conv_ln_b']),
        'w_conv_out': p['w_conv_out'].astype(BF16),
        'cmp_w1bd': _pair_block_diag(p['cmp_w1']).astype(BF16),
        'cmp_w2bd': _pair_block_diag(p['cmp_w2']).astype(BF16),
        'cmp_w1pair': jnp.einsum('scde,bk->sdbcke', p['cmp_w1'], jnp.eye(4, dtype=F32)).reshape(
            2, HEAD_DIM // 2, 2 * PAGE_SIZE, 4 * CMP_HIDDEN).astype(BF16),
        'cmp_w2bd4': jnp.einsum('sed,bk->sbekd', p['cmp_w2'], jnp.eye(4, dtype=F32)).reshape(
            2, 4 * CMP_HIDDEN, 4 * HEAD_DIM).astype(BF16),
        'cmp_pe4': jnp.tile(jnp.einsum('scd,scde->se', p['cmp_pe'], p['cmp_w1'], precision=lax.Precision.HIGHEST), (1, 4)),
        'w_attn_out': p['w_attn_out'].astype(BF16),
        'w_o': p['w_o'].astype(BF16),
        'g_ffn': row(p['g_ffn']),
        'peer_wq': p['peer_wq'].astype(BF16),
        'peer_subkeys': p['peer_subkeys'].reshape(2 * PEER_HEADS, N_KEYS, PEER_KEY_DIM // 2).astype(BF16),
        'cmp_pe2': jnp.tile(jnp.einsum('scd,scde->se', p['cmp_pe'], p['cmp_w1'], precision=lax.Precision.HIGHEST), (1, 2)),
    }


def _pair_block_diag(a):
    z = jnp.zeros_like(a)
    return jnp.concatenate([jnp.concatenate([a, z], axis=-1), jnp.concatenate([z, a], axis=-1)], axis=-2)


CONV_HALO = 32
CONV_CHUNK = 64


def _ln_silu_project(y, lng, lnb, wout):
    mu = jnp.mean(y, axis=-1, keepdims=True)
    yc = y - mu
    var = jnp.mean(yc * yc, axis=-1, keepdims=True)
    yn = yc * lax.rsqrt(var + NORM_EPS) * lng + lnb
    act = yn * _sigmoid(yn)
    return _dot(act.astype(BF16), wout)


def _conv_prompt_kernel(glu_ref, cw_ref, cb_ref, lng_ref, lnb_ref, wout_ref, out_ref, ext_ref, y_ref):
    t = pl.program_id(1)
    tt = glu_ref.shape[1]

    @pl.when(t == 0)
    def _():
        ext_ref[0:CONV_HALO, :] = jnp.zeros((CONV_HALO, D_CONV), F32)

    @pl.when(t > 0)
    def _():
        ext_ref[0:CONV_HALO, :] = ext_ref[tt:tt + CONV_HALO, :]

    ext_ref[CONV_HALO:CONV_HALO + tt, :] = glu_ref[0]
    off = CONV_HALO - (CONV_WIDTH - 1)
    for r0 in range(0, tt, CONV_CHUNK):
        acc = jnp.zeros((CONV_CHUNK, D_CONV), F32)
        for j in range(CONV_WIDTH):
            acc = acc + ext_ref[r0 + off + j:r0 + off + j + CONV_CHUNK, :] * cw_ref[j:j + 1, :]
        y_ref[r0:r0 + CONV_CHUNK, :] = acc + cb_ref[...]
    out_ref[0] = _ln_silu_project(y_ref[...], lng_ref[...], lnb_ref[...], wout_ref[...])


def _conv_prompt(glu, w, t_tile):
    n, t, _ = glu.shape
    consts = (w['conv_w'], w['conv_b'], w['conv_ln_g'], w['conv_ln_b'], w['w_conv_out'])
    return pl.pallas_call(
        _conv_prompt_kernel,
        grid=(n, t // t_tile),
        in_specs=[pl.BlockSpec((1, t_tile, D_CONV), lambda i, j: (i, j, 0))] + [_const_spec(c.shape) for c in consts],
        out_specs=pl.BlockSpec((1, t_tile, D_MODEL), lambda i, j: (i, j, 0)),
        out_shape=jax.ShapeDtypeStruct((n, t, D_MODEL), F32),
        scratch_shapes=[pltpu.VMEM((t_tile + CONV_HALO, D_CONV), F32), pltpu.VMEM((t_tile, D_CONV), F32)],
        compiler_params=_params(("parallel", "arbitrary")),
        name="conv_prompt",
    )(glu, *consts)


def _conv_step_kernel(st_ref, gl_ref, cw_ref, cb_ref, lng_ref, lnb_ref, wout_ref, out_ref):
    n_buf = st_ref.shape[0]
    for t in range(gl_ref.shape[0]):
        acc = jnp.zeros(gl_ref.shape[1:], F32)
        for j in range(CONV_WIDTH):
            k = t + j
            row = st_ref[k] if k < n_buf else gl_ref[k - n_buf]
            acc = acc + row * cw_ref[j:j + 1, :]
        out_ref[t] = _ln_silu_project(acc + cb_ref[...], lng_ref[...], lnb_ref[...], wout_ref[...])


def _conv_step(st_tm, gl_tm, w):
    t, b, _ = gl_tm.shape
    consts = (w['conv_w'], w['conv_b'], w['conv_ln_g'], w['conv_ln_b'], w['w_conv_out'])
    return pl.pallas_call(
        _conv_step_kernel,
        grid=(1,),
        in_specs=[_const_spec(st_tm.shape), _const_spec(gl_tm.shape)] + [_const_spec(c.shape) for c in consts],
        out_specs=_const_spec((t, b, D_MODEL)),
        out_shape=jax.ShapeDtypeStruct((t, b, D_MODEL), F32),
        compiler_params=_params(("arbitrary",)),
        name="conv_step",
    )(st_tm, gl_tm, *consts)


BLOCK_COLS = CMP_BLOCK * KV_ROW
PAGE_SIZE = 128
PAGE_BLOCKS = PAGE_SIZE // CMP_BLOCK


def _gelu(x):
    return 0.5 * x * (1.0 + lax.erf(x * (2.0 ** -0.5)))


def _compress_rows(x_ref, m, w1_ref, pe_ref, w2_ref, kg_ref):
    outs = []
    for s in range(2):
        parts = []
        for hp in range(N_KV_HEADS // 2):
            acc = jnp.zeros((m, 2 * CMP_HIDDEN), F32)
            for c in range(CMP_BLOCK):
                col = c * KV_ROW + s * (KV_ROW // 2) + hp * LANES
                xs = x_ref[:, pl.ds(col, LANES)]
                acc = acc + _dot(xs.astype(BF16), w1_ref[s, c])
            hdn = _gelu(acc + pe_ref[s:s + 1, :])
            parts.append(_dot(hdn.astype(BF16), w2_ref[s]))
        outs.append(parts)
    ones_bd = _head_pair_ones()
    kc = []
    for part in outs[0]:
        hi, lo = _split_bf16(part * part)
        ss = _dot(hi, ones_bd) + _dot(lo, ones_bd)
        kc.append(part * lax.rsqrt(ss * (1.0 / HEAD_DIM) + NORM_EPS) * kg_ref[...])
    return jnp.concatenate(kc, axis=1), jnp.concatenate(outs[1], axis=1)


def _compress_prompt_kernel(x_ref, w1_ref, pe_ref, w2_ref, kg_ref, kc_ref, vc_ref):
    kc, vc = _compress_rows(x_ref, kc_ref.shape[0], w1_ref, pe_ref, w2_ref, kg_ref)
    kc_ref[...] = kc
    vc_ref[...] = vc


def _compress_prompt(ckv2d, w, m_tile):
    nblk = ckv2d.shape[0] // CMP_BLOCK
    x = ckv2d.reshape(nblk, BLOCK_COLS)
    consts = (w['cmp_w1bd'], w['cmp_pe2'], w['cmp_w2bd'], w['k_gain2'][0:1])
    out = pl.BlockSpec((m_tile, KV_ROW // 2), lambda i: (i, 0))
    return pl.pallas_call(
        _compress_prompt_kernel,
        grid=(nblk // m_tile,),
        in_specs=[pl.BlockSpec((m_tile, BLOCK_COLS), lambda i: (i, 0))] + [_const_spec(c.shape) for c in consts],
        out_specs=[out, out],
        out_shape=[jax.ShapeDtypeStruct((nblk, KV_ROW // 2), F32)] * 2,
        compiler_params=_params(("parallel",)),
        name="compress_prompt",
    )(x, *consts)


NSA_TQ = 256
CMP_PER_SEL = SEL_BLOCK // CMP_BLOCK
SEL_PER_TILE = NSA_TQ // SEL_BLOCK
WIN_TILES = WINDOW // NSA_TQ + 1
FLASH_ROWS = 256


def _dup_halves(x, in_head_half):
    return jnp.where(in_head_half, x, pltpu.roll(x, HEAD_DIM, 1))


def _rank_select(score, n_rows):
    row = lax.broadcasted_iota(jnp.int32, (n_rows, 1), 0)
    rank = jnp.zeros(score.shape, F32)
    for b in range(n_rows):
        sb = score[b:b + 1, :]
        tie_wins = jnp.where(row > b, 1.0, 0.0)
        rank = rank + jnp.where(sb > score, 1.0, jnp.where(sb == score, tie_wins, 0.0))
    return jnp.where(rank < SEL_TOPK, 1.0, 0.0)


def _flash_tile(qm, kd, v_lo, v_hi, mask, m, l, acc):
    lower = lax.broadcasted_iota(jnp.int32, (1, LANES), 1) < HEAD_DIM
    tq = mask.shape[0]
    m_rows = [[] for _ in range(GROUP)]
    l_rows = [[] for _ in range(GROUP)]
    acc_rows = [[] for _ in range(GROUP // 2)]
    for r0 in range(0, tq, FLASH_ROWS):
        rs = slice(r0, r0 + FLASH_ROWS)
        alphas, pvs = [], []
        for g in range(GROUP):
            s = jnp.where(mask[rs], _dot_nt(qm[g][rs], kd), NEG_INF)
            mg = jnp.maximum(m[g][rs], jnp.max(s, axis=-1, keepdims=True))
            alpha = jnp.exp(m[g][rs] - mg)
            p = jnp.exp(s - mg)
            l_rows[g].append(alpha * l[g][rs] + jnp.sum(p, axis=-1, keepdims=True))
            m_rows[g].append(mg)
            alphas.append(alpha)
            pvs.append(_dot(p.astype(BF16), v_lo if g % 2 == 0 else v_hi))
        for j in range(GROUP // 2):
            acc_rows[j].append(acc[j][rs] * jnp.where(lower, alphas[2 * j], alphas[2 * j + 1]) + pvs[2 * j] + pvs[2 * j + 1])
    cat = lambda parts: jnp.concatenate(parts, axis=0)
    return [cat(x) for x in m_rows], [cat(x) for x in l_rows], [cat(x) for x in acc_rows]


def _nsa_prompt_kernel(q_ref, kc_ref, vc_ref, ks_ref, vs_ref, kw0_ref, kw1_ref, kw2_ref, vw0_ref, vw1_ref, vw2_ref,
                       bg0_ref, bg1_ref, bg2_ref, o_ref):
    qi = pl.program_id(1)
    h = pl.program_id(2)
    tq = q_ref.shape[0]
    lane = lax.broadcasted_iota(jnp.int32, (1, LANES), 1)
    lower = lane < HEAD_DIM
    upper = lane >= HEAD_DIM
    in_head_half = (lane // HEAD_DIM) == (h % 2)
    pos = qi * tq + lax.broadcasted_iota(jnp.int32, (tq, 1), 0)

    q = q_ref[...]
    zero = jnp.zeros((tq, LANES), BF16)
    qm = [jnp.where(lower if g % 2 == 0 else upper, q[:, (g // 2) * LANES:(g // 2 + 1) * LANES], zero)
          for g in range(GROUP)]

    kcd = _dup_halves(kc_ref[...], in_head_half).astype(BF16)
    vcd = _dup_halves(vc_ref[...], in_head_half)
    vc_lo = jnp.where(lower, vcd, 0.0).astype(BF16)
    vc_hi = jnp.where(lower, 0.0, vcd).astype(BF16)
    nc = kc_ref.shape[0]
    blk_end = (lax.broadcasted_iota(jnp.int32, (1, nc), 1) + 1) * CMP_BLOCK - 1
    cmask = blk_end <= pos
    imp = jnp.zeros((tq, nc), F32)
    o_cmp = [jnp.zeros((tq, LANES), F32) for _ in range(GROUP // 2)]
    for g in range(GROUP):
        s = jnp.where(cmask, _dot_nt(qm[g], kcd), NEG_INF)
        e = jnp.where(cmask, jnp.exp(s - jnp.max(s, axis=-1, keepdims=True)), 0.0)
        pr = e / jnp.maximum(jnp.sum(e, axis=-1, keepdims=True), TINY)
        imp = imp + pr
        o_cmp[g // 2] = o_cmp[g // 2] + _dot(pr.astype(BF16), vc_lo if g % 2 == 0 else vc_hi)

    n_sel = nc // CMP_PER_SEL
    pair_t = jnp.where(lax.broadcasted_iota(jnp.int32, (n_sel, nc), 1) // CMP_PER_SEL
                       == lax.broadcasted_iota(jnp.int32, (n_sel, nc), 0), 1.0, 0.0).astype(BF16)
    imp_hi, imp_lo = _split_bf16(imp)
    imp_t = _dot_nt(pair_t, imp_hi) + _dot_nt(pair_t, imp_lo)
    pos_t = qi * tq + lax.broadcasted_iota(jnp.int32, (1, tq), 1)
    blk = lax.broadcasted_iota(jnp.int32, (n_sel, 1), 0)
    cur = pos_t // SEL_BLOCK
    forced = (blk == 0) | (blk == cur) | (blk == cur - 1)
    score = jnp.where(blk * SEL_BLOCK <= pos_t, imp_t + jnp.where(forced, FORCE_BONUS, 0.0), NEG_INF)
    sel_t = _rank_select(score, n_sel).astype(BF16)

    blk_of_key = lax.broadcasted_iota(jnp.int32, (n_sel, tq), 1) // SEL_BLOCK
    blk_row = lax.broadcasted_iota(jnp.int32, (n_sel, tq), 0)
    key_off = lax.broadcasted_iota(jnp.int32, (1, tq), 1)

    def sel_body(kt, carry):
        m, l, acc = carry
        start = pl.multiple_of(kt * tq, tq)
        kd = ks_ref[pl.ds(start, tq), :]
        vd = vs_ref[pl.ds(start, tq), :]
        expand = jnp.where(blk_of_key + kt * SEL_PER_TILE == blk_row, 1.0, 0.0).astype(BF16)
        mask = (_dot_tn(sel_t, expand) > 0.5) & (kt * tq + key_off <= pos)
        return _flash_tile(qm, kd, jnp.where(lower, vd, zero), jnp.where(lower, zero, vd), mask, m, l, acc)

    init = ([jnp.full((tq, 1), NEG_INF, F32)] * GROUP, [jnp.zeros((tq, 1), F32)] * GROUP,
            [jnp.zeros((tq, LANES), F32)] * (GROUP // 2))
    m, l, acc = lax.fori_loop(0, qi + 1, sel_body, init)
    o_sel = [acc[j] / jnp.where(lower, l[2 * j], l[2 * j + 1]) for j in range(GROUP // 2)]

    m, l, acc = init
    for d, (kw_ref, vw_ref) in enumerate(((kw0_ref, vw0_ref), (kw1_ref, vw1_ref), (kw2_ref, vw2_ref))):
        kpos = (qi - d) * tq + key_off
        diff = pos - kpos
        mask = (diff >= 0) & (diff < WINDOW) & (kpos >= 0)
        vd = vw_ref[...]
        m, l, acc = _flash_tile(qm, kw_ref[...], jnp.where(lower, vd, zero), jnp.where(lower, zero, vd), mask, m, l, acc)
    o_win = [acc[j] / jnp.where(lower, l[2 * j], l[2 * j + 1]) for j in range(GROUP // 2)]

    for j in range(GROUP // 2):
        sl = slice(j * LANES, (j + 1) * LANES)
        o = bg0_ref[:, sl] * o_cmp[j] + bg1_ref[:, sl] * o_sel[j] + bg2_ref[:, sl] * o_win[j]
        o_ref[:, sl] = o.astype(BF16)


def _nsa_prompt(q, kc, vc, kdup_s, vdup_s, kdup_w, vdup_w, bgx, n, t):
    tq = NSA_TQ
    nq = t // tq
    nc = t // CMP_BLOCK
    hw = GROUP * HEAD_DIM
    qspec = pl.BlockSpec((tq, hw), lambda i, j, h: (i * nq + j, h))
    cspec = pl.BlockSpec((nc, LANES), lambda i, j, h: (i, h // 2))
    seq_spec = pl.BlockSpec((t, LANES), lambda i, j, h: (i, h))
    win_specs = [pl.BlockSpec((tq, LANES), functools.partial(lambda d, i, j, h: (i * nq + jnp.maximum(j - d, 0), h), d))
                 for d in range(WIN_TILES)]
    bg_specs = [pl.BlockSpec((tq, hw), functools.partial(lambda b, i, j, h: (i * nq + j, b * N_KV_HEADS + h), b))
                for b in range(3)]
    return pl.pallas_call(
        _nsa_prompt_kernel,
        grid=(n, nq, N_KV_HEADS),
        in_specs=[qspec, cspec, cspec, seq_spec, seq_spec] + win_specs + win_specs + bg_specs,
        out_specs=qspec,
        out_shape=jax.ShapeDtypeStruct((n * t, C_Q), BF16),
        compiler_params=_params(("parallel", "parallel", "arbitrary")),
        name="nsa_prompt",
    )(q, kc, vc, kdup_s, vdup_s, kdup_w, kdup_w, kdup_w, vdup_w, vdup_w, vdup_w, bgx, bgx, bgx)


HD = N_KV_HEADS * HEAD_DIM
CP_PAGES = 32
BLOCKS_PER_PAGE = PAGE_SIZE // CMP_BLOCK
DEC_T = 4
STEP_ROWS = N_KV_HEADS * DEC_T * GROUP
SUB_PAGES = 8
PAGES_PER_SEQ = 64
NEW_ROWS = 8
LOWEST = -3.0e38


def _native_pool(cache_layer):
    n_pool = cache_layer.shape[0]
    return jnp.transpose(cache_layer, (0, 2, 3, 4, 1)).reshape(n_pool, 2, HD, PAGE_SIZE)


def _slab_copy(pool_ref, pt_ref, buf_ref, sem, seq, first_page, slot, i, s):
    page = pt_ref[seq, first_page + i]
    dst = buf_ref.at[slot, pl.ds((s * CP_PAGES + i) * HD, HD), :]
    return pltpu.make_async_copy(pool_ref.at[page, s], dst, sem.at[slot])


def _fetch_pages(pool_ref, pt_ref, buf_ref, sem, seq, first_page, slot, wait):
    for i in range(CP_PAGES):
        for s in range(2):
            cp = _slab_copy(pool_ref, pt_ref, buf_ref, sem, seq, first_page, slot, i, s)
            if wait:
                cp.wait()
            else:
                cp.start()


def _compress_pages_kernel(pt_ref, pool_ref, w1_hbm, pe_ref, w2_ref, kg_ref, kc_ref, vc_ref, buf_ref, w1_ref, sem, wsem):
    step = pl.program_id(0)
    n_steps = pl.num_programs(0)
    halves = pt_ref.shape[1] // CP_PAGES
    fetch = functools.partial(_fetch_pages, pool_ref, pt_ref, buf_ref, sem)

    @pl.when(step == 0)
    def _():
        cp = pltpu.make_async_copy(w1_hbm, w1_ref, wsem)
        cp.start()
        fetch(0, 0, 0, False)
        cp.wait()

    for slot in range(2):
        @pl.when(step % 2 == slot)
        def _():
            @pl.when(step + 1 < n_steps)
            def _():
                nxt = step + 1
                fetch(nxt // halves, (nxt % halves) * CP_PAGES, 1 - slot, False)

            fetch(step // halves, (step % halves) * CP_PAGES, slot, True)
            ones_bd = _head_pair_ones()
            for s, out_ref in ((0, kc_ref), (1, vc_ref)):
                acc = jnp.zeros((N_KV_HEADS * CP_PAGES, BLOCKS_PER_PAGE * CMP_HIDDEN), F32)
                for dp in range(HEAD_DIM // 2):
                    lanes = []
                    for d in (2 * dp, 2 * dp + 1):
                        rows = [buf_ref[slot, pl.ds(s * CP_PAGES * HD + h * HEAD_DIM + d, CP_PAGES, stride=HD), :]
                                for h in range(N_KV_HEADS)]
                        lanes.append(jnp.concatenate(rows, axis=0))
                    acc = acc + _dot(jnp.concatenate(lanes, axis=1).astype(BF16), w1_ref[s, dp])
                hdn = _gelu(acc + pe_ref[s:s + 1, :])
                out = _dot(hdn.astype(BF16), w2_ref[s])
                if s == 0:
                    tiles = []
                    for j in range(out.shape[1] // LANES):
                        part = out[:, j * LANES:(j + 1) * LANES]
                        hi, lo = _split_bf16(part * part)
                        ss = _dot(hi, ones_bd) + _dot(lo, ones_bd)
                        tiles.append(part * lax.rsqrt(ss * (1.0 / HEAD_DIM) + NORM_EPS) * kg_ref[...])
                    out = jnp.concatenate(tiles, axis=1)
                for h in range(N_KV_HEADS):
                    out_ref[0, h] = out[h * CP_PAGES:(h + 1) * CP_PAGES, :]


def _compress_pages(pool_t, page_table, w):
    n_seq, n_pages = page_table.shape
    halves = n_pages // CP_PAGES
    consts = (w['cmp_pe4'], w['cmp_w2bd4'], w['k_gain2'][0:1])
    out = pl.BlockSpec((1, N_KV_HEADS, CP_PAGES, BLOCKS_PER_PAGE * HEAD_DIM), lambda i, pt: (i // halves, 0, i % halves, 0))
    grid_spec = pltpu.PrefetchScalarGridSpec(
        num_scalar_prefetch=1,
        grid=(n_seq * halves,),
        in_specs=[pl.BlockSpec(memory_space=pl.ANY), pl.BlockSpec(memory_space=pl.ANY)]
        + [pl.BlockSpec(c.shape, functools.partial(lambda nd, i, pt: (0,) * nd, len(c.shape))) for c in consts],
        out_specs=[out, out],
        scratch_shapes=[pltpu.VMEM((2, 2 * CP_PAGES * HD, PAGE_SIZE), F32), pltpu.VMEM(w['cmp_w1pair'].shape, BF16),
                        pltpu.SemaphoreType.DMA((2,)), pltpu.SemaphoreType.DMA(())],
    )
    shape = jax.ShapeDtypeStruct((n_seq, N_KV_HEADS, n_pages, BLOCKS_PER_PAGE * HEAD_DIM), F32)
    return pl.pallas_call(
        _compress_pages_kernel,
        grid_spec=grid_spec,
        out_shape=[shape, shape],
        compiler_params=_params(("arbitrary",)),
        name="compress_pages",
    )(page_table, pool_t, w['cmp_w1pair'], *consts)


def _diag_heads(o):
    rows = STEP_ROWS // N_KV_HEADS
    parts = []
    for h in range(N_KV_HEADS):
        tile = o[h * rows:(h + 1) * rows, (h // 2) * LANES:(h // 2 + 1) * LANES]
        parts.append(pltpu.roll(tile, HEAD_DIM, 1) if h % 2 else tile)
    return jnp.concatenate(parts, axis=0)


def _topk_lanes(score, score_new, idx, idx_new, k):
    sel = jnp.zeros(score.shape, F32)
    sel_new = jnp.zeros(score_new.shape, F32)
    big = jnp.float32(1.0e9)
    for _ in range(k):
        mx = jnp.maximum(jnp.max(score, axis=-1, keepdims=True), score_new)
        first = jnp.minimum(jnp.min(jnp.where(score == mx, idx, big), axis=-1, keepdims=True),
                            jnp.where(score_new == mx, idx_new, big))
        pick = idx == first
        pick_new = first == idx_new
        sel = jnp.where(pick, 1.0, sel)
        sel_new = jnp.where(pick_new, 1.0, sel_new)
        score = jnp.where(pick, LOWEST, score)
        score_new = jnp.where(pick_new, LOWEST, score_new)
    return sel, sel_new


def _nsa_step_kernel(pt_ref, q_ref, gate_ref, kc_ref, vc_ref, snew_ref, wnew_ref, wnewt_ref, win_ref, exp_ref, pool_ref,
                     x_ref, wout_ref, buf_ref, sem):
    seq = pl.program_id(0)
    n_seq = pl.num_programs(0)
    n_pages = pt_ref.shape[1]
    past_len = n_pages * PAGE_SIZE
    n_past_blk = past_len // SEL_BLOCK
    fetch = functools.partial(_fetch_pages, pool_ref, pt_ref, buf_ref, sem)

    @pl.when(seq == 0)
    def _():
        fetch(0, 0, 0, False)

    fetch(seq, CP_PAGES, 1, False)

    q = q_ref[0]
    row = lax.broadcasted_iota(jnp.int32, (STEP_ROWS, 1), 0)
    t_row = (row // GROUP) % DEC_T
    qpos = past_len + t_row
    lane = lax.broadcasted_iota(jnp.int32, (1, LANES), 1)
    lower = lane < HEAD_DIM
    rows_h = STEP_ROWS // N_KV_HEADS

    c_even = BLOCKS_PER_PAGE * (lane % PAGES_PER_SEQ) + 2 * (lane // PAGES_PER_SEQ)
    group_sum = jnp.where(lax.broadcasted_iota(jnp.int32, (rows_h, rows_h), 0) // GROUP
                          == lax.broadcasted_iota(jnp.int32, (rows_h, rows_h), 1) // GROUP, 1.0, 0.0).astype(BF16)
    o_cmp, imp = [], []
    for h in range(N_KV_HEADS):
        qh = q[h * rows_h:(h + 1) * rows_h, (h // 2) * LANES:(h // 2 + 1) * LANES]
        q_lo, q_hi = (pltpu.roll(qh.astype(F32), HEAD_DIM, 1).astype(BF16), qh) if h % 2 else (qh, pltpu.roll(qh.astype(F32), HEAD_DIM, 1).astype(BF16))
        kcat = jnp.concatenate([kc_ref[0, h, :, 0:LANES], kc_ref[0, h, :, LANES:2 * LANES]], axis=0).astype(BF16)
        vcat = jnp.concatenate([vc_ref[0, h, :, 0:LANES], vc_ref[0, h, :, LANES:2 * LANES]], axis=0)
        qp = qpos[h * rows_h:(h + 1) * rows_h]
        m_even = (c_even + 1) * CMP_BLOCK - 1 <= qp
        m_odd = (c_even + 2) * CMP_BLOCK - 1 <= qp
        s_even = jnp.where(m_even, _dot_nt(q_lo, kcat), NEG_INF)
        s_odd = jnp.where(m_odd, _dot_nt(q_hi, kcat), NEG_INF)
        mx = jnp.maximum(jnp.max(s_even, axis=-1, keepdims=True), jnp.max(s_odd, axis=-1, keepdims=True))
        e_even = jnp.where(m_even, jnp.exp(s_even - mx), 0.0)
        e_odd = jnp.where(m_odd, jnp.exp(s_odd - mx), 0.0)
        den = jnp.maximum(jnp.sum(e_even, axis=-1, keepdims=True) + jnp.sum(e_odd, axis=-1, keepdims=True), TINY)
        pr_even = e_even / den
        pr_odd = e_odd / den
        oc = (_dot(pr_even.astype(BF16), jnp.where(lower, vcat, 0.0).astype(BF16))
              + _dot(pr_odd.astype(BF16), jnp.where(lower, 0.0, vcat).astype(BF16)))
        o_cmp.append(oc + pltpu.roll(oc, HEAD_DIM, 1))
        p_hi, p_lo = _split_bf16(pr_even + pr_odd)
        imp.append(_dot(group_sum, p_hi) + _dot(group_sum, p_lo))
    o_cmp = jnp.concatenate(o_cmp, axis=0)
    imp = jnp.concatenate(imp, axis=0)

    blk = 2 * (lane % PAGES_PER_SEQ) + lane // PAGES_PER_SEQ
    cur = qpos // SEL_BLOCK
    forced = (blk == 0) | (blk == cur) | (blk == cur - 1)
    score = jnp.where(blk * SEL_BLOCK <= qpos, imp + jnp.where(forced, FORCE_BONUS, 0.0), NEG_INF)
    forced_new = (n_past_blk == cur) | (n_past_blk == cur - 1)
    score_new = jnp.where(n_past_blk * SEL_BLOCK <= qpos, jnp.where(forced_new, FORCE_BONUS, 0.0), NEG_INF)
    sel, sel_new = _topk_lanes(score, score_new, blk.astype(F32), jnp.float32(n_past_blk), SEL_TOPK)
    bias_rows = jnp.where(sel > 0.5, 0.0, NEG_INF).astype(BF16)

    def page_tiles(slot, page0, carry):
        m, l, acc = carry
        for sc in range(CP_PAGES // SUB_PAGES):
            pages = [sc * SUB_PAGES + i for i in range(SUB_PAGES)]
            s_t = jnp.concatenate([_dot(q, buf_ref[slot, pl.ds(i * HD, HD), :].astype(BF16)) for i in pages], axis=1)
            first_key = (page0 + sc * SUB_PAGES) * PAGE_SIZE
            s_t = s_t + _dot(bias_rows, exp_ref[:, first_key:first_key + SUB_PAGES * PAGE_SIZE])
            m_new = jnp.maximum(m, jnp.max(s_t, axis=-1, keepdims=True))
            alpha = jnp.exp(m - m_new)
            p = jnp.exp(s_t - m_new)
            l = alpha * l + jnp.sum(p, axis=-1, keepdims=True)
            acc = acc * alpha
            for k, i in enumerate(pages):
                v = buf_ref[slot, pl.ds((CP_PAGES + i) * HD, HD), :].astype(BF16)
                acc = acc + _dot_nt(p[:, k * PAGE_SIZE:(k + 1) * PAGE_SIZE].astype(BF16), v)
            m = m_new
        return m, l, acc

    carry = (jnp.full((STEP_ROWS, 1), NEG_INF, F32), jnp.zeros((STEP_ROWS, 1), F32), jnp.zeros((STEP_ROWS, HD), F32))
    fetch(seq, 0, 0, True)
    carry = page_tiles(0, 0, carry)

    @pl.when(seq + 1 < n_seq)
    def _():
        fetch(seq + 1, 0, 0, False)

    fetch(seq, CP_PAGES, 1, True)
    m, l, acc = page_tiles(1, CP_PAGES, carry)

    new_vis = lax.broadcasted_iota(jnp.int32, (1, NEW_ROWS), 1) <= t_row
    snew = snew_ref[0]
    s_n = jnp.where(new_vis & (sel_new > 0.5), _dot_nt(q, snew[:, 0:HD].astype(BF16)), NEG_INF)
    m_new = jnp.maximum(m, jnp.max(s_n, axis=-1, keepdims=True))
    alpha = jnp.exp(m - m_new)
    p_n = jnp.exp(s_n - m_new)
    l = alpha * l + jnp.sum(p_n, axis=-1, keepdims=True)
    acc = acc * alpha + _dot(p_n.astype(BF16), snew[:, HD:2 * HD].astype(BF16))
    o_sel = _diag_heads(acc / l)

    lb = win_ref.shape[3]
    kpos = past_len - lb + lax.broadcasted_iota(jnp.int32, (1, lb), 1)
    diff = qpos - kpos
    s_w = jnp.where((diff >= 0) & (diff < WINDOW) & (kpos >= 0), _dot(q, win_ref[0, 0].astype(BF16)), NEG_INF)
    wnew = wnew_ref[0]
    s_wn = jnp.where(new_vis, _dot_nt(q, wnew[:, 0:HD].astype(BF16)), NEG_INF)
    mw = jnp.maximum(jnp.max(s_w, axis=-1, keepdims=True), jnp.max(s_wn, axis=-1, keepdims=True))
    p_w = jnp.exp(s_w - mw)
    p_wn = jnp.exp(s_wn - mw)
    l_w = jnp.sum(p_w, axis=-1, keepdims=True) + jnp.sum(p_wn, axis=-1, keepdims=True)
    acc_w = _dot_nt(p_w.astype(BF16), win_ref[0, 1].astype(BF16)) + _dot(p_wn.astype(BF16), wnew[:, HD:2 * HD].astype(BF16))
    o_win = _diag_heads(acc_w / l_w)

    gate = gate_ref[0]
    x_ref[0] = gate[:, 0:1] * o_cmp + gate[:, 1:2] * o_sel + gate[:, 2:3] * o_win

    shift = LANES - DEC_T
    keep = lane < shift
    n_tiles = lb // LANES
    for s in range(2):
        for k in range(n_tiles):
            cur_tile = pltpu.roll(win_ref[0, s, :, k * LANES:(k + 1) * LANES], shift, 1)
            nxt_tile = (pltpu.roll(win_ref[0, s, :, (k + 1) * LANES:(k + 2) * LANES], shift, 1) if k + 1 < n_tiles
                        else wnewt_ref[0, s])
            wout_ref[0, s, :, k * LANES:(k + 1) * LANES] = jnp.where(keep, cur_tile, nxt_tile)


def _nsa_step(q_bd, gates, kc, vc, snew8, wnew8, wnew_t, win_t, expand, pool_t, page_table):
    n_seq, n_pages = page_table.shape
    lb = win_t.shape[3]
    per_seq = lambda shape: pl.BlockSpec((1,) + shape, functools.partial(lambda nd, i, pt: (i,) + (0,) * nd, len(shape)))
    cshape = (N_KV_HEADS, n_pages, BLOCKS_PER_PAGE * HEAD_DIM)
    grid_spec = pltpu.PrefetchScalarGridSpec(
        num_scalar_prefetch=1,
        grid=(n_seq,),
        in_specs=[per_seq((STEP_ROWS, HD)), per_seq((STEP_ROWS, LANES)), per_seq(cshape), per_seq(cshape),
                  per_seq((NEW_ROWS, KV_ROW)), per_seq((NEW_ROWS, KV_ROW)), per_seq((2, HD, LANES)), per_seq((2, HD, lb)),
                  pl.BlockSpec(expand.shape, lambda i, pt: (0, 0)), pl.BlockSpec(memory_space=pl.ANY)],
        out_specs=[per_seq((STEP_ROWS, LANES)), per_seq((2, HD, lb))],
        scratch_shapes=[pltpu.VMEM((2, 2 * CP_PAGES * HD, PAGE_SIZE), F32), pltpu.SemaphoreType.DMA((2,))],
    )
    return pl.pallas_call(
        _nsa_step_kernel,
        grid_spec=grid_spec,
        out_shape=[jax.ShapeDtypeStruct((n_seq, STEP_ROWS, LANES), F32), jax.ShapeDtypeStruct((n_seq, 2, HD, lb), F32)],
        compiler_params=_params(("arbitrary",)),
        name="nsa_step",
    )(page_table, q_bd, gates, kc, vc, snew8, wnew8, wnew_t, win_t, expand, pool_t)


def _step_query_layout(q_s, n_seq):
    q5 = q_s.reshape(n_seq, DEC_T, N_KV_HEADS, GROUP, HEAD_DIM)
    qt = jnp.transpose(q5, (0, 2, 1, 3, 4)).reshape(n_seq, N_KV_HEADS, DEC_T * GROUP, HEAD_DIM)
    eye = jnp.eye(N_KV_HEADS, dtype=q_s.dtype)
    return jnp.einsum('nhrd,hk->nhrkd', qt, eye).reshape(n_seq, STEP_ROWS, HD)


def _step_gate_layout(bgx_s, n_seq):
    bg = bgx_s.reshape(n_seq, DEC_T, 3, N_KV_HEADS, GROUP, HEAD_DIM)[..., 0]
    bg = jnp.transpose(bg, (0, 3, 1, 4, 2)).reshape(n_seq, STEP_ROWS, 3)
    return jnp.pad(bg, ((0, 0), (0, 0), (0, LANES - 3)))


def _step_output_layout(x, n_seq):
    x5 = x[:, :, :HEAD_DIM].reshape(n_seq, N_KV_HEADS, DEC_T, GROUP, HEAD_DIM)
    return jnp.transpose(x5, (0, 2, 1, 3, 4)).reshape(n_seq * DEC_T, C_Q)


def _block_expand_matrix(n_pages):
    rows = jnp.arange(LANES)
    keys = jnp.arange(n_pages * PAGE_SIZE)
    hit = (((rows[:, None] % PAGES_PER_SEQ) == (keys[None, :] // PAGE_SIZE))
           & ((rows[:, None] // PAGES_PER_SEQ) == ((keys[None, :] % PAGE_SIZE) // SEL_BLOCK)))
    return hit.astype(BF16)


PEER_HALF = PEER_KEY_DIM // 2
N_SIDES = 2 * PEER_HEADS


def _merge_kernel(x_ref, conv_ref, o_ref, gmix_ref, wmg_ref, wao_ref, wo_ref, gffn_ref, wq_ref, sk_ref,
                  h_ref, hn_ref, st_ref):
    x = x_ref[...]
    n = (x * lax.rsqrt(jnp.mean(x * x, axis=-1, keepdims=True) + NORM_EPS) * gmix_ref[...]).astype(BF16)
    mg = _sigmoid(_dot(n, wmg_ref[...]))
    attn = _dot(o_ref[...], wao_ref[...])
    mix = mg[:, :D_MODEL] * conv_ref[...] + mg[:, D_MODEL:] * attn
    h = x + _dot(mix.astype(BF16), wo_ref[...])
    h_ref[...] = h
    hn = (h * lax.rsqrt(jnp.mean(h * h, axis=-1, keepdims=True) + NORM_EPS) * gffn_ref[...]).astype(BF16)
    hn_ref[...] = hn
    qp = _dot(hn, wq_ref[...]).astype(BF16)
    for i in range(N_SIDES):
        st_ref[i * N_KEYS:(i + 1) * N_KEYS, :] = _dot_nt(sk_ref[i], qp[:, i * PEER_HALF:(i + 1) * PEER_HALF])


def _merge(x2d, conv_out, o, w, row_tile):
    ntok = x2d.shape[0]
    row = lambda width: pl.BlockSpec((row_tile, width), lambda i: (i, 0))
    consts = (w['g_mix'], w['w_mg'], w['w_attn_out'], w['w_o'], w['g_ffn'], w['peer_wq'], w['peer_subkeys'])
    return pl.pallas_call(
        _merge_kernel,
        grid=(ntok // row_tile,),
        in_specs=[row(D_MODEL), row(D_MODEL), row(C_Q)] + [_const_spec(c.shape) for c in consts],
        out_specs=[row(D_MODEL), row(D_MODEL), pl.BlockSpec((N_SIDES * N_KEYS, row_tile), lambda i: (0, i))],
        out_shape=[jax.ShapeDtypeStruct((ntok, D_MODEL), F32), jax.ShapeDtypeStruct((ntok, D_MODEL), BF16),
                   jax.ShapeDtypeStruct((N_SIDES * N_KEYS, ntok), F32)],
        compiler_params=_params(("parallel",)),
        name="merge",
    )(x2d, conv_out, o, *consts)


def _cand_groups():
    groups = []
    for j in range(PEER_TOPK):
        n_valid = PEER_TOPK // (j + 1)
        groups.append((j, n_valid, -(-n_valid // 8) * 8))
    return groups


def _top_rows_sorted(s, k):
    n = s.shape[0]
    row = lax.broadcasted_iota(jnp.int32, (n, 1), 0).astype(F32)
    out_row = lax.broadcasted_iota(jnp.int32, (k, 1), 0)
    vals = jnp.zeros((k, s.shape[1]), F32)
    idxs = jnp.zeros((k, s.shape[1]), F32)
    for r in range(k):
        mx = jnp.max(s, axis=0, keepdims=True)
        first = jnp.min(jnp.where(s == mx, row, float(n)), axis=0, keepdims=True)
        vals = jnp.where(out_row == r, mx, vals)
        idxs = jnp.where(out_row == r, first, idxs)
        s = jnp.where(row == first, LOWEST, s)
    return vals, idxs


def _peer_topk_kernel(st_ref, a_ref, b_ref, g_ref):
    tt = st_ref.shape[1]
    groups = _cand_groups()
    row16 = lax.broadcasted_iota(jnp.int32, (PEER_TOPK, 1), 0).astype(F32)
    a_rows, b_rows, g_rows = [], [], []
    for hd in range(PEER_HEADS):
        va, ia = _top_rows_sorted(st_ref[(2 * hd) * N_KEYS:(2 * hd + 1) * N_KEYS, :], PEER_TOPK)
        vb, ib = _top_rows_sorted(st_ref[(2 * hd + 1) * N_KEYS:(2 * hd + 2) * N_KEYS, :], PEER_TOPK)
        cands, flats = [], []
        for j, n_valid, n_rows in groups:
            i_idx = lax.broadcasted_iota(jnp.int32, (n_rows, 1), 0)
            cands.append(jnp.where(i_idx < n_valid, va[0:n_rows, :] + vb[j:j + 1, :], LOWEST))
            flats.append((i_idx * PEER_TOPK + j).astype(F32))
        cand = jnp.concatenate(cands, axis=0)
        flat = jnp.concatenate(flats, axis=0)
        n_flat = float(PEER_TOPK * PEER_TOPK)
        sc = jnp.zeros((PEER_TOPK, tt), F32)
        ea = jnp.zeros((PEER_TOPK, tt), F32)
        eb = jnp.zeros((PEER_TOPK, tt), F32)
        for r in range(PEER_TOPK):
            mx = jnp.max(cand, axis=0, keepdims=True)
            first = jnp.min(jnp.where(cand == mx, flat, n_flat), axis=0, keepdims=True)
            cand = jnp.where(flat == first, LOWEST, cand)
            i_sel = jnp.floor(first * (1.0 / PEER_TOPK))
            j_sel = first - i_sel * PEER_TOPK
            a_sel = jnp.sum(jnp.where(row16 == i_sel, ia, 0.0), axis=0, keepdims=True)
            b_sel = jnp.sum(jnp.where(row16 == j_sel, ib, 0.0), axis=0, keepdims=True)
            sc = jnp.where(row16 == r, mx, sc)
            ea = jnp.where(row16 == r, a_sel, ea)
            eb = jnp.where(row16 == r, b_sel, eb)
        e = jnp.exp(sc - sc[0:1, :])
        g_rows.append(e / jnp.sum(e, axis=0, keepdims=True))
        a_rows.append(ea)
        b_rows.append(eb)
    a_ref[...] = jnp.transpose(jnp.concatenate(a_rows, axis=0))
    b_ref[...] = jnp.transpose(jnp.concatenate(b_rows, axis=0))
    g_ref[...] = jnp.transpose(jnp.concatenate(g_rows, axis=0))


def _peer_topk(scores_t, tok_tile):
    ntok = scores_t.shape[1]
    n_pick = PEER_HEADS * PEER_TOPK
    out = pl.BlockSpec((tok_tile, n_pick), lambda i: (i, 0))
    return pl.pallas_call(
        _peer_topk_kernel,
        grid=(ntok // tok_tile,),
        in_specs=[pl.BlockSpec((N_SIDES * N_KEYS, tok_tile), lambda i: (0, i))],
        out_specs=[out, out, out],
        out_shape=[jax.ShapeDtypeStruct((ntok, n_pick), F32)] * 3,
        compiler_params=_params(("parallel",)),
        name="peer_topk",
    )(scores_t)


W_TOK = 64


def _peer_w_kernel(a_ref, b_ref, g_ref, w_ref):
    sub = lax.broadcasted_iota(jnp.int32, (N_KEYS, 1), 0).astype(F32)

    def body(t, carry):
        a_row = a_ref[pl.ds(t, 1), :]
        b_row = b_ref[pl.ds(t, 1), :]
        g_row = g_ref[pl.ds(t, 1), :]
        ga = jnp.where(sub == a_row, g_row, 0.0).astype(BF16)
        ob = jnp.where(sub == b_row, 1.0, 0.0).astype(BF16)
        w_ref[t] = _dot_nt(ga, ob)
        return carry

    lax.fori_loop(0, W_TOK, body, 0, unroll=4)


def _peer_weights(a_idx, b_idx, gate):
    ntok, n_pick = a_idx.shape
    row = pl.BlockSpec((W_TOK, n_pick), lambda i: (i, 0))
    return pl.pallas_call(
        _peer_w_kernel,
        grid=(ntok // W_TOK,),
        in_specs=[row, row, row],
        out_specs=pl.BlockSpec((W_TOK, N_KEYS, N_KEYS), lambda i: (i, 0, 0)),
        out_shape=jax.ShapeDtypeStruct((ntok, N_KEYS, N_KEYS), F32),
        compiler_params=_params(("parallel",)),
        name="peer_weights",
    )(a_idx, b_idx, gate)


PEER_TOK = 512
PEER_ABLK = 16


def _peer_dense_kernel(hn_ref, h_ref, w_ref, u_ref, v_ref, y_ref, acc_ref):
    j = pl.program_id(1)

    @pl.when(j == 0)
    def _():
        acc_ref[...] = jnp.zeros(acc_ref.shape, F32)

    hn = hn_ref[...]
    pair = 2 * N_KEYS
    for k in range(PEER_ABLK // 2):
        act = _dot_nt(hn, u_ref[k * pair:(k + 1) * pair, :])
        wk = jnp.concatenate([w_ref[:, 2 * k, :], w_ref[:, 2 * k + 1, :]], axis=1)
        acc_ref[...] += _dot((wk * _gelu(act)).astype(BF16), v_ref[k * pair:(k + 1) * pair, :])

    @pl.when(j == pl.num_programs(1) - 1)
    def _():
        y_ref[...] = h_ref[...] + acc_ref[...]


def _peer_dense(hn, h, w_atb, u, v):
    ntok = hn.shape[0]
    blk = PEER_ABLK * N_KEYS
    tok = lambda: pl.BlockSpec((PEER_TOK, D_MODEL), lambda i, j: (i, 0))
    exp = lambda: pl.BlockSpec((blk, D_MODEL), lambda i, j: (j, 0))
    return pl.pallas_call(
        _peer_dense_kernel,
        grid=(ntok // PEER_TOK, N_KEYS // PEER_ABLK),
        in_specs=[tok(), tok(), pl.BlockSpec((PEER_TOK, PEER_ABLK, N_KEYS), lambda i, j: (i, j, 0)), exp(), exp()],
        out_specs=tok(),
        out_shape=jax.ShapeDtypeStruct((ntok, D_MODEL), F32),
        scratch_shapes=[pltpu.VMEM((PEER_TOK, D_MODEL), F32)],
        compiler_params=_params(("parallel", "arbitrary")),
        name="peer_dense",
    )(hn, h, w_atb, u, v)


def kernel(x_prompt, x_sample, cache_cmp_kv, cache_sel_kv, state_win_kv, state_conv, page_table, g_mix, w_in, conv_w, conv_b, conv_ln_g, conv_ln_b, w_conv_out, q_gain, k_gain, cmp_pe, cmp_w1, cmp_w2, w_attn_out, w_o, g_ffn, peer_wq, peer_subkeys, peer_u, peer_v):
    layer = 0
    p = {'g_mix': g_mix, 'w_in': w_in, 'conv_w': conv_w, 'conv_b': conv_b, 'conv_ln_g': conv_ln_g,
         'conv_ln_b': conv_ln_b, 'w_conv_out': w_conv_out, 'q_gain': q_gain, 'k_gain': k_gain, 'cmp_pe': cmp_pe,
         'cmp_w1': cmp_w1, 'cmp_w2': cmp_w2, 'w_attn_out': w_attn_out, 'w_o': w_o, 'g_ffn': g_ffn,
         'peer_wq': peer_wq, 'peer_subkeys': peer_subkeys}
    w = _prep_weights({k: v[layer] for k, v in p.items()})
    u_bf = peer_u[layer].astype(BF16)
    v_bf = peer_v[layer].astype(BF16)

    def ffn(x2d, conv_out, o):
        h, hn, scores_t = _merge(x2d, conv_out, o, w, ROW_TILE)
        a_idx, b_idx, gate = _peer_topk(scores_t, ROW_TILE)
        return _peer_dense(hn, h, _peer_weights(a_idx, b_idx, gate), u_bf, v_bf)

    n, t, _ = x_prompt.shape
    xp = x_prompt.reshape(n * t, D_MODEL)
    cos_p, sin_p = _rope_tables(jnp.arange(t, dtype=jnp.int32))
    glu, q, ckv, skv, wkv, bgx, kds, vds, kdw, vdw = _project(xp, cos_p, sin_p, w, ROW_TILE)
    glu3 = glu.reshape(n, t, D_CONV)
    conv_p = _conv_prompt(glu3, w, ROW_TILE).reshape(n * t, D_MODEL)
    kc, vc = _compress_prompt(ckv, w, 128)
    o_p = _nsa_prompt(q, kc, vc, kds, vds, kdw, vdw, bgx, n, t)
    y_p = ffn(xp, conv_p, o_p).reshape(n, t, D_MODEL)

    n_s, t_s, _ = x_sample.shape
    past_len = page_table.shape[1] * PAGE_SIZE
    xs = x_sample.reshape(n_s * t_s, D_MODEL)
    pos_s = past_len + jnp.arange(t_s, dtype=jnp.int32)
    cos_s, sin_s = _rope_tables(jnp.tile(pos_s, ROW_TILE // t_s))
    glu_s, q_s, ckv_s, skv_s, wkv_s, bgx_s = _project(xs, cos_s, sin_s, w, ROW_TILE)[:6]
    glu_s3 = glu_s.reshape(n_s, t_s, D_CONV)
    conv_s = _conv_step(jnp.swapaxes(state_conv[layer], 0, 1), jnp.swapaxes(glu_s3, 0, 1), w)
    conv_s = jnp.swapaxes(conv_s, 0, 1).reshape(n_s * t_s, D_MODEL)
    assert t_s == DEC_T and page_table.shape[1] == PAGES_PER_SEQ
    kc_s, vc_s = _compress_pages(_native_pool(cache_cmp_kv[layer]), page_table, w)
    lb = state_win_kv.shape[2]
    win_t = jnp.transpose(state_win_kv[layer], (0, 2, 3, 4, 1)).reshape(n_s, 2, HD, lb)
    pad_new = lambda a: jnp.pad(a.reshape(n_s, t_s, KV_ROW), ((0, 0), (0, NEW_ROWS - t_s), (0, 0)))
    wnew_t = jnp.pad(jnp.transpose(wkv_s.reshape(n_s, t_s, 2, HD), (0, 2, 3, 1)), ((0, 0), (0, 0), (0, 0), (LANES - t_s, 0)))
    x_step, win_s = _nsa_step(_step_query_layout(q_s, n_s), _step_gate_layout(bgx_s, n_s), kc_s, vc_s,
                              pad_new(skv_s), pad_new(wkv_s), wnew_t, win_t, _block_expand_matrix(PAGES_PER_SEQ),
                              _native_pool(cache_sel_kv[layer]), page_table)
    o_s = _step_output_layout(x_step, n_s).astype(BF16)
    y_s = ffn(xs, conv_s, o_s).reshape(n_s, t_s, D_MODEL)
    win_s = jnp.transpose(win_s.reshape(n_s, 2, N_KV_HEADS, HEAD_DIM, lb), (0, 4, 1, 2, 3))

    kv6 = lambda a, nn, tt: a.reshape(1, nn, tt, 2, N_KV_HEADS, HEAD_DIM)
    n_win = min(WINDOW, t)
    new_conv_s = jnp.concatenate([state_conv[layer][:, t_s:], glu_s3], axis=1)
    return (y_p, y_s,
            kv6(ckv, n, t), kv6(skv, n, t), kv6(wkv, n, t)[:, :, t - n_win:], glu3[None, :, t - (CONV_WIDTH - 1):],
            kv6(ckv_s, n_s, t_s), kv6(skv_s, n_s, t_s), win_s[None], new_conv_s[None])
```

```python
import functools
import math

import jax
import jax.numpy as jnp
from jax import lax
from jax.experimental import pallas as pl
from jax.experimental.pallas import tpu as pltpu

F32 = jnp.float32
BF16 = jnp.bfloat16

D_MODEL = 1024
D_CONV = 512
CONV_WIDTH = 31
N_HEADS = 16
N_KV_HEADS = 4
HEAD_DIM = 64
GROUP = N_HEADS // N_KV_HEADS
CMP_BLOCK = 32
CMP_HIDDEN = 128
SEL_BLOCK = 64
SEL_TOPK = 16
WINDOW = 512
ROPE_THETA = 10000.0
FORCE_BONUS = 1.0e3
N_KEYS = 128
PEER_HEADS = 8
PEER_TOPK = 16
PEER_KEY_DIM = 256
NORM_EPS = 1e-6
NEG_INF = -1.0e30
TINY = 1.0e-30
ATTN_SCALE = HEAD_DIM ** -0.5
KV_ROW = 2 * N_KV_HEADS * HEAD_DIM
C_Q = N_HEADS * HEAD_DIM

ROW_TILE = 256
LANES = 128
VMEM_LIMIT = 56 * 1024 * 1024


def _dot(a, b):
    return jnp.dot(a, b, preferred_element_type=F32)


def _dot_nt(a, b):
    return lax.dot_general(a, b, (((1,), (1,)), ((), ())), preferred_element_type=F32)


def _dot_tn(a, b):
    return lax.dot_general(a, b, (((0,), (0,)), ((), ())), preferred_element_type=F32)


def _split_bf16(x):
    hi = x.astype(BF16)
    lo = (x - hi.astype(F32)).astype(BF16)
    return hi, lo


def _sigmoid(x):
    return 1.0 / (1.0 + jnp.exp(-x))


def _params(sem):
    return pltpu.CompilerParams(dimension_semantics=sem, vmem_limit_bytes=VMEM_LIMIT)


def _const_spec(shape):
    nd = len(shape)
    return pl.BlockSpec(shape, lambda *_: (0,) * nd)


def _head_pair_ones():
    r = lax.broadcasted_iota(jnp.int32, (LANES, LANES), 0) // HEAD_DIM
    c = lax.broadcasted_iota(jnp.int32, (LANES, LANES), 1) // HEAD_DIM
    return jnp.where(r == c, 1.0, 0.0).astype(BF16)


def _head_rms_rope(z, gain, cos, sin_signed, ones_bd, first_half):
    hi, lo = _split_bf16(z * z)
    ss = _dot(hi, ones_bd) + _dot(lo, ones_bd)
    zn = z * lax.rsqrt(ss * (1.0 / HEAD_DIM) + NORM_EPS) * gain
    partner = jnp.where(first_half, pltpu.roll(zn, LANES - HEAD_DIM // 2, 1), pltpu.roll(zn, HEAD_DIM // 2, 1))
    return zn * cos + partner * sin_signed


def _proj_kernel(x_ref, cos_ref, sin_ref, gmix_ref, wglu_ref, wq_ref, wkv_ref, wbg_ref, qg_ref, kg_ref,
                 glu_ref, q_ref, ckv_ref, skv_ref, wkvo_ref, bg_ref, *dup_refs):
    x = x_ref[...]
    ms = jnp.mean(x * x, axis=-1, keepdims=True)
    n = (x * lax.rsqrt(ms + NORM_EPS) * gmix_ref[...]).astype(BF16)
    cos = cos_ref[...]
    sin_signed = sin_ref[...]
    ones_bd = _head_pair_ones()
    lane = lax.broadcasted_iota(jnp.int32, (1, LANES), 1)
    first_half = (lane % HEAD_DIM) < (HEAD_DIM // 2)

    zg = _dot(n, wglu_ref[...])
    glu_ref[...] = zg[:, :D_CONV] * _sigmoid(zg[:, D_CONV:])

    zq = _dot(n, wq_ref[...])
    qg = qg_ref[...]
    for j in range(C_Q // LANES):
        sl = slice(j * LANES, (j + 1) * LANES)
        qj = _head_rms_rope(zq[:, sl], qg, cos, sin_signed, ones_bd, first_half)
        q_ref[:, sl] = (qj * ATTN_SCALE).astype(BF16)

    zkv = _dot(n, wkv_ref[...])
    half = KV_ROW // 2
    lower = lane < HEAD_DIM
    for b, out_ref in enumerate((ckv_ref, skv_ref, wkvo_ref)):
        kg = kg_ref[b:b + 1, :]
        for j in range(half // LANES):
            sl = slice(b * KV_ROW + j * LANES, b * KV_ROW + (j + 1) * LANES)
            kj = _head_rms_rope(zkv[:, sl], kg, cos, sin_signed, ones_bd, first_half)
            out_ref[:, j * LANES:(j + 1) * LANES] = kj
            vj = zkv[:, b * KV_ROW + half + j * LANES:b * KV_ROW + half + (j + 1) * LANES]
            if b > 0:
                for src, dup_ref in ((kj, dup_refs[2 * (b - 1)]), (vj, dup_refs[2 * (b - 1) + 1])):
                    swapped = pltpu.roll(src, HEAD_DIM, 1)
                    dup_ref[:, 2 * j * LANES:(2 * j + 1) * LANES] = jnp.where(lower, src, swapped).astype(BF16)
                    dup_ref[:, (2 * j + 1) * LANES:(2 * j + 2) * LANES] = jnp.where(lower, swapped, src).astype(BF16)
        out_ref[:, half:] = zkv[:, b * KV_ROW + half:(b + 1) * KV_ROW]

    bg_ref[...] = _sigmoid(_dot(n, wbg_ref[...]))


def _rope_tables(pos):
    half = HEAD_DIM // 2
    inv = jnp.exp(-(2.0 * math.log(ROPE_THETA) / HEAD_DIM) * jnp.arange(half, dtype=F32))
    ang = pos.astype(F32)[:, None] * inv[None, :]
    lane = jnp.arange(LANES)
    cos = jnp.cos(ang)[:, lane % half]
    sin = jnp.sin(ang)[:, lane % half]
    sin_signed = jnp.where((lane % HEAD_DIM) < half, -sin, sin)
    return cos, sin_signed


def _project(x2d, cos, sin_signed, w, row_tile):
    ntok = x2d.shape[0]
    n_pos_blocks = cos.shape[0] // row_tile
    row = lambda width: pl.BlockSpec((row_tile, width), lambda i: (i, 0))
    tab = pl.BlockSpec((row_tile, LANES), lambda i: (i % n_pos_blocks, 0))
    consts = (w['g_mix'], w['w_glu'], w['w_q'], w['w_kv'], w['w_bgx'], w['q_gain2'], w['k_gain2'])
    return pl.pallas_call(
        _proj_kernel,
        grid=(ntok // row_tile,),
        in_specs=[row(D_MODEL), tab, tab] + [_const_spec(c.shape) for c in consts],
        out_specs=[row(D_CONV), row(C_Q), row(KV_ROW), row(KV_ROW), row(KV_ROW), row(3 * C_Q)] + [row(KV_ROW)] * 4,
        out_shape=[jax.ShapeDtypeStruct((ntok, D_CONV), F32), jax.ShapeDtypeStruct((ntok, C_Q), BF16),
                   jax.ShapeDtypeStruct((ntok, KV_ROW), F32), jax.ShapeDtypeStruct((ntok, KV_ROW), F32),
                   jax.ShapeDtypeStruct((ntok, KV_ROW), F32), jax.ShapeDtypeStruct((ntok, 3 * C_Q), F32)]
        + [jax.ShapeDtypeStruct((ntok, KV_ROW), BF16)] * 4,
        compiler_params=_params(("parallel",)),
        name="proj",
    )(x2d, cos, sin_signed, *consts)


def _prep_weights(p):
    w_in = p['w_in']
    c_glu = 2 * D_CONV
    c_kv = 3 * KV_ROW
    c_bg = 3 * N_HEADS
    o1, o2, o3 = c_glu, c_glu + C_Q, c_glu + C_Q + c_kv
    o4 = o3 + c_bg
    row = lambda v: v.reshape(1, -1)
    return {
        'g_mix': row(p['g_mix']),
        'w_glu': w_in[:, :o1].astype(BF16),
        'w_q': w_in[:, o1:o2].astype(BF16),
        'w_kv': w_in[:, o2:o3].astype(BF16),
        'w_bgx': jnp.repeat(w_in[:, o3:o4], HEAD_DIM, axis=1).astype(BF16),
        'w_mg': w_in[:, o4:].astype(BF16),
        'q_gain2': row(jnp.tile(p['q_gain'], 2)),
        'k_gain2': jnp.tile(p['k_gain'], (1, 2)),
        'conv_w': p['conv_w'],
        'conv_b': row(p['conv_b']),
        'conv_ln_g': row(p['conv_ln_g']),
        'conv_ln_b': row(p['conv_ln_b']),
        'w_conv_out': p['w_conv_out'].astype(BF16),
        'cmp_w1bd': _pair_block_diag(p['cmp_w1']).astype(BF16),
        'cmp_w2bd': _pair_block_diag(p['cmp_w2']).astype(BF16),
        'cmp_w1pair': jnp.einsum('scde,bk->sdbcke', p['cmp_w1'], jnp.eye(4, dtype=F32)).reshape(
            2, HEAD_DIM // 2, 2 * PAGE_SIZE, 4 * CMP_HIDDEN).astype(BF16),
        'cmp_w2bd4': jnp.einsum('sed,bk->sbekd', p['cmp_w2'], jnp.eye(4, dtype=F32)).reshape(
            2, 4 * CMP_HIDDEN, 4 * HEAD_DIM).astype(BF16),
        'cmp_pe4': jnp.tile(jnp.einsum('scd,scde->se', p['cmp_pe'], p['cmp_w1'], precision=lax.Precision.HIGHEST), (1, 4)),
        'w_attn_out': p['w_attn_out'].astype(BF16),
        'w_o': p['w_o'].astype(BF16),
        'g_ffn': row(p['g_ffn']),
        'peer_wq': p['peer_wq'].astype(BF16),
        'peer_subkeys': p['peer_subkeys'].reshape(2 * PEER_HEADS, N_KEYS, PEER_KEY_DIM // 2).astype(BF16),
        'cmp_pe2': jnp.tile(jnp.einsum('scd,scde->se', p['cmp_pe'], p['cmp_w1'], precision=lax.Precision.HIGHEST), (1, 2)),
    }


def _pair_block_diag(a):
    z = jnp.zeros_like(a)
    return jnp.concatenate([jnp.concatenate([a, z], axis=-1), jnp.concatenate([z, a], axis=-1)], axis=-2)


CONV_HALO = 32
CONV_CHUNK = 64


def _ln_silu_project(y, lng, lnb, wout):
    mu = jnp.mean(y, axis=-1, keepdims=True)
    yc = y - mu
    var = jnp.mean(yc * yc, axis=-1, keepdims=True)
    yn = yc * lax.rsqrt(var + NORM_EPS) * lng + lnb
    act = yn * _sigmoid(yn)
    return _dot(act.astype(BF16), wout)


def _conv_prompt_kernel(glu_ref, cw_ref, cb_ref, lng_ref, lnb_ref, wout_ref, out_ref, ext_ref, y_ref):
    t = pl.program_id(1)
    tt = glu_ref.shape[1]

    @pl.when(t == 0)
    def _():
        ext_ref[0:CONV_HALO, :] = jnp.zeros((CONV_HALO, D_CONV), F32)

    @pl.when(t > 0)
    def _():
        ext_ref[0:CONV_HALO, :] = ext_ref[tt:tt + CONV_HALO, :]

    ext_ref[CONV_HALO:CONV_HALO + tt, :] = glu_ref[0]
    off = CONV_HALO - (CONV_WIDTH - 1)
    for r0 in range(0, tt, CONV_CHUNK):
        acc = jnp.zeros((CONV_CHUNK, D_CONV), F32)
        for j in range(CONV_WIDTH):
            acc = acc + ext_ref[r0 + off + j:r0 + off + j + CONV_CHUNK, :] * cw_ref[j:j + 1, :]
        y_ref[r0:r0 + CONV_CHUNK, :] = acc + cb_ref[...]
    out_ref[0] = _ln_silu_project(y_ref[...], lng_ref[...], lnb_ref[...], wout_ref[...])


def _conv_prompt(glu, w, t_tile):
    n, t, _ = glu.shape
    consts = (w['conv_w'], w['conv_b'], w['conv_ln_g'], w['conv_ln_b'], w['w_conv_out'])
    return pl.pallas_call(
        _conv_prompt_kernel,
        grid=(n, t // t_tile),
        in_specs=[pl.BlockSpec((1, t_tile, D_CONV), lambda i, j: (i, j, 0))] + [_const_spec(c.shape) for c in consts],
        out_specs=pl.BlockSpec((1, t_tile, D_MODEL), lambda i, j: (i, j, 0)),
        out_shape=jax.ShapeDtypeStruct((n, t, D_MODEL), F32),
        scratch_shapes=[pltpu.VMEM((t_tile + CONV_HALO, D_CONV), F32), pltpu.VMEM((t_tile, D_CONV), F32)],
        compiler_params=_params(("parallel", "arbitrary")),
        name="conv_prompt",
    )(glu, *consts)


def _conv_step_kernel(st_ref, gl_ref, cw_ref, cb_ref, lng_ref, lnb_ref, wout_ref, out_ref):
    n_buf = st_ref.shape[0]
    for t in range(gl_ref.shape[0]):
        acc = jnp.zeros(gl_ref.shape[1:], F32)
        for j in range(CONV_WIDTH):
            k = t + j
            row = st_ref[k] if k < n_buf else gl_ref[k - n_buf]
            acc = acc + row * cw_ref[j:j + 1, :]
        out_ref[t] = _ln_silu_project(acc + cb_ref[...], lng_ref[...], lnb_ref[...], wout_ref[...])


def _conv_step(st_tm, gl_tm, w):
    t, b, _ = gl_tm.shape
    consts = (w['conv_w'], w['conv_b'], w['conv_ln_g'], w['conv_ln_b'], w['w_conv_out'])
    return pl.pallas_call(
        _conv_step_kernel,
        grid=(1,),
        in_specs=[_const_spec(st_tm.shape), _const_spec(gl_tm.shape)] + [_const_spec(c.shape) for c in consts],
        out_specs=_const_spec((t, b, D_MODEL)),
        out_shape=jax.ShapeDtypeStruct((t, b, D_MODEL), F32),
        compiler_params=_params(("arbitrary",)),
        name="conv_step",
    )(st_tm, gl_tm, *consts)


BLOCK_COLS = CMP_BLOCK * KV_ROW
PAGE_SIZE = 128
PAGE_BLOCKS = PAGE_SIZE // CMP_BLOCK


def _gelu(x):
    return 0.5 * x * (1.0 + lax.erf(x * (2.0 ** -0.5)))


def _compress_rows(x_ref, m, w1_ref, pe_ref, w2_ref, kg_ref):
    outs = []
    for s in range(2):
        parts = []
        for hp in range(N_KV_HEADS // 2):
            acc = jnp.zeros((m, 2 * CMP_HIDDEN), F32)
            for c in range(CMP_BLOCK):
                col = c * KV_ROW + s * (KV_ROW // 2) + hp * LANES
                xs = x_ref[:, pl.ds(col, LANES)]
                acc = acc + _dot(xs.astype(BF16), w1_ref[s, c])
            hdn = _gelu(acc + pe_ref[s:s + 1, :])
            parts.append(_dot(hdn.astype(BF16), w2_ref[s]))
        outs.append(parts)
    ones_bd = _head_pair_ones()
    kc = []
    for part in outs[0]:
        hi, lo = _split_bf16(part * part)
        ss = _dot(hi, ones_bd) + _dot(lo, ones_bd)
        kc.append(part * lax.rsqrt(ss * (1.0 / HEAD_DIM) + NORM_EPS) * kg_ref[...])
    return jnp.concatenate(kc, axis=1), jnp.concatenate(outs[1], axis=1)


def _compress_prompt_kernel(x_ref, w1_ref, pe_ref, w2_ref, kg_ref, kc_ref, vc_ref):
    kc, vc = _compress_rows(x_ref, kc_ref.shape[0], w1_ref, pe_ref, w2_ref, kg_ref)
    kc_ref[...] = kc
    vc_ref[...] = vc


def _compress_prompt(ckv2d, w, m_tile):
    nblk = ckv2d.shape[0] // CMP_BLOCK
    x = ckv2d.reshape(nblk, BLOCK_COLS)
    consts = (w['cmp_w1bd'], w['cmp_pe2'], w['cmp_w2bd'], w['k_gain2'][0:1])
    out = pl.BlockSpec((m_tile, KV_ROW // 2), lambda i: (i, 0))
    return pl.pallas_call(
        _compress_prompt_kernel,
        grid=(nblk // m_tile,),
        in_specs=[pl.BlockSpec((m_tile, BLOCK_COLS), lambda i: (i, 0))] + [_const_spec(c.shape) for c in consts],
        out_specs=[out, out],
        out_shape=[jax.ShapeDtypeStruct((nblk, KV_ROW // 2), F32)] * 2,
        compiler_params=_params(("parallel",)),
        name="compress_prompt",
    )(x, *consts)


NSA_TQ = 256
CMP_PER_SEL = SEL_BLOCK // CMP_BLOCK
SEL_PER_TILE = NSA_TQ // SEL_BLOCK
WIN_TILES = WINDOW // NSA_TQ + 1
FLASH_ROWS = 256


def _dup_halves(x, in_head_half):
    return jnp.where(in_head_half, x, pltpu.roll(x, HEAD_DIM, 1))


def _rank_select(score, n_rows):
    row = lax.broadcasted_iota(jnp.int32, (n_rows, 1), 0)
    rank = jnp.zeros(score.shape, F32)
    for b in range(n_rows):
        sb = score[b:b + 1, :]
        tie_wins = jnp.where(row > b, 1.0, 0.0)
        rank = rank + jnp.where(sb > score, 1.0, jnp.where(sb == score, tie_wins, 0.0))
    return jnp.where(rank < SEL_TOPK, 1.0, 0.0)


def _flash_tile(q4, kd, vd, bias, m, l, acc):
    tq = bias.shape[0]
    lower = lax.broadcasted_iota(jnp.int32, (1, LANES), 1) < HEAD_DIM
    one = jnp.ones(vd.shape, BF16)
    v2 = jnp.concatenate([jnp.where(lower, vd, one), jnp.where(lower, one, vd)], axis=1)
    s = _dot_nt(q4, kd).reshape(GROUP, tq, vd.shape[0]) + bias[None]
    m_new = jnp.maximum(m, jnp.max(s, axis=-1, keepdims=True))
    alpha = jnp.exp(m - m_new)
    p = jnp.exp(s - m_new).astype(BF16).reshape(GROUP * tq, vd.shape[0])
    r = _dot(p, v2).reshape(GROUP, tq, 2 * LANES)
    sums = jnp.stack([r[g, :, HEAD_DIM:HEAD_DIM + 1] if g % 2 == 0 else r[g, :, LANES:LANES + 1] for g in range(GROUP)])
    new_acc = [acc[j] * jnp.where(lower, alpha[2 * j], alpha[2 * j + 1])
               + jnp.where(lower, r[2 * j, :, 0:LANES], r[2 * j + 1, :, LANES:2 * LANES]) for j in range(GROUP // 2)]
    return m_new, alpha * l + sums, new_acc


def _nsa_prompt_kernel(q_ref, kc_ref, vc_ref, ks_ref, vs_ref, kw0_ref, kw1_ref, kw2_ref, vw0_ref, vw1_ref, vw2_ref,
                       bg0_ref, bg1_ref, bg2_ref, o_ref):
    qi = pl.program_id(1)
    h = pl.program_id(2)
    tq = q_ref.shape[0]
    lane = lax.broadcasted_iota(jnp.int32, (1, LANES), 1)
    lower = lane < HEAD_DIM
    upper = lane >= HEAD_DIM
    in_head_half = (lane // HEAD_DIM) == (h % 2)
    pos = qi * tq + lax.broadcasted_iota(jnp.int32, (tq, 1), 0)

    q = q_ref[...]
    zero = jnp.zeros((tq, LANES), BF16)
    qm = [jnp.where(lower if g % 2 == 0 else upper, q[:, (g // 2) * LANES:(g // 2 + 1) * LANES], zero)
          for g in range(GROUP)]

    kcd = _dup_halves(kc_ref[...], in_head_half).astype(BF16)
    vcd = _dup_halves(vc_ref[...], in_head_half)
    vc_lo = jnp.where(lower, vcd, 0.0).astype(BF16)
    vc_hi = jnp.where(lower, 0.0, vcd).astype(BF16)
    nc = kc_ref.shape[0]
    blk_end = (lax.broadcasted_iota(jnp.int32, (1, nc), 1) + 1) * CMP_BLOCK - 1
    cmask = blk_end <= pos
    imp = jnp.zeros((tq, nc), F32)
    o_cmp = [jnp.zeros((tq, LANES), F32) for _ in range(GROUP // 2)]
    for g in range(GROUP):
        s = jnp.where(cmask, _dot_nt(qm[g], kcd), NEG_INF)
        e = jnp.where(cmask, jnp.exp(s - jnp.max(s, axis=-1, keepdims=True)), 0.0)
        pr = e / jnp.maximum(jnp.sum(e, axis=-1, keepdims=True), TINY)
        imp = imp + pr
        o_cmp[g // 2] = o_cmp[g // 2] + _dot(pr.astype(BF16), vc_lo if g % 2 == 0 else vc_hi)

    n_sel = nc // CMP_PER_SEL
    pair_t = jnp.where(lax.broadcasted_iota(jnp.int32, (n_sel, nc), 1) // CMP_PER_SEL
                       == lax.broadcasted_iota(jnp.int32, (n_sel, nc), 0), 1.0, 0.0).astype(BF16)
    imp_hi, imp_lo = _split_bf16(imp)
    imp_t = _dot_nt(pair_t, imp_hi) + _dot_nt(pair_t, imp_lo)
    pos_t = qi * tq + lax.broadcasted_iota(jnp.int32, (1, tq), 1)
    blk = lax.broadcasted_iota(jnp.int32, (n_sel, 1), 0)
    cur = pos_t // SEL_BLOCK
    forced = (blk == 0) | (blk == cur) | (blk == cur - 1)
    score = jnp.where(blk * SEL_BLOCK <= pos_t, imp_t + jnp.where(forced, FORCE_BONUS, 0.0), NEG_INF)
    sel_t = _rank_select(score, n_sel).astype(BF16)

    not_sel = (1.0 - sel_t.astype(F32)).astype(BF16)
    blk_of_key = lax.broadcasted_iota(jnp.int32, (n_sel, tq), 1) // SEL_BLOCK
    blk_row = lax.broadcasted_iota(jnp.int32, (n_sel, tq), 0)
    key_off = lax.broadcasted_iota(jnp.int32, (1, tq), 1)
    q4 = jnp.concatenate(qm, axis=0)

    def sel_body(kt, carry):
        m, l, acc = carry
        start = pl.multiple_of(kt * tq, tq)
        tags = jnp.where(blk_of_key + kt * SEL_PER_TILE == blk_row, NEG_INF, 0.0).astype(BF16)
        bias = _dot_tn(not_sel, tags) + jnp.where(kt * tq + key_off <= pos, 0.0, NEG_INF)
        return _flash_tile(q4, ks_ref[pl.ds(start, tq), :], vs_ref[pl.ds(start, tq), :], bias, m, l, acc)

    init = (jnp.full((GROUP, tq, 1), NEG_INF, F32), jnp.zeros((GROUP, tq, 1), F32),
            [jnp.zeros((tq, LANES), F32)] * (GROUP // 2))
    m, l, acc = lax.fori_loop(0, qi + 1, sel_body, init)
    o_sel = [acc[j] / jnp.where(lower, l[2 * j], l[2 * j + 1]) for j in range(GROUP // 2)]

    kpos = jnp.concatenate([(qi - d) * tq + key_off for d in range(WIN_TILES)], axis=1)
    diff = pos - kpos
    bias = jnp.where((diff >= 0) & (diff < WINDOW) & (kpos >= 0), 0.0, NEG_INF)
    kw = jnp.concatenate([kw0_ref[...], kw1_ref[...], kw2_ref[...]], axis=0)
    vw = jnp.concatenate([vw0_ref[...], vw1_ref[...], vw2_ref[...]], axis=0)
    m, l, acc = _flash_tile(q4, kw, vw, bias, *init)
    o_win = [acc[j] / jnp.where(lower, l[2 * j], l[2 * j + 1]) for j in range(GROUP // 2)]

    for j in range(GROUP // 2):
        sl = slice(j * LANES, (j + 1) * LANES)
        o = bg0_ref[:, sl] * o_cmp[j] + bg1_ref[:, sl] * o_sel[j] + bg2_ref[:, sl] * o_win[j]
        o_ref[:, sl] = o.astype(BF16)


def _nsa_prompt(q, kc, vc, kdup_s, vdup_s, kdup_w, vdup_w, bgx, n, t):
    tq = NSA_TQ
    nq = t // tq
    nc = t // CMP_BLOCK
    hw = GROUP * HEAD_DIM
    qspec = pl.BlockSpec((tq, hw), lambda i, j, h: (i * nq + j, h))
    cspec = pl.BlockSpec((nc, LANES), lambda i, j, h: (i, h // 2))
    seq_spec = pl.BlockSpec((t, LANES), lambda i, j, h: (i, h))
    win_specs = [pl.BlockSpec((tq, LANES), functools.partial(lambda d, i, j, h: (i * nq + jnp.maximum(j - d, 0), h), d))
                 for d in range(WIN_TILES)]
    bg_specs = [pl.BlockSpec((tq, hw), functools.partial(lambda b, i, j, h: (i * nq + j, b * N_KV_HEADS + h), b))
                for b in range(3)]
    return pl.pallas_call(
        _nsa_prompt_kernel,
        grid=(n, nq, N_KV_HEADS),
        in_specs=[qspec, cspec, cspec, seq_spec, seq_spec] + win_specs + win_specs + bg_specs,
        out_specs=qspec,
        out_shape=jax.ShapeDtypeStruct((n * t, C_Q), BF16),
        compiler_params=_params(("parallel", "parallel", "arbitrary")),
        name="nsa_prompt",
    )(q, kc, vc, kdup_s, vdup_s, kdup_w, kdup_w, kdup_w, vdup_w, vdup_w, vdup_w, bgx, bgx, bgx)


HD = N_KV_HEADS * HEAD_DIM
CP_PAGES = 32
BLOCKS_PER_PAGE = PAGE_SIZE // CMP_BLOCK
DEC_T = 4
STEP_ROWS = N_KV_HEADS * DEC_T * GROUP
SUB_PAGES = 32
PAGES_PER_SEQ = 64
NEW_ROWS = 8
LOWEST = -3.0e38


def _native_pool(cache_layer):
    n_pool = cache_layer.shape[0]
    return jnp.transpose(cache_layer, (0, 2, 3, 4, 1)).reshape(n_pool, 2, HD, PAGE_SIZE)


def _slab_copy(pool_ref, pt_ref, buf_ref, sem, seq, first_page, slot, i, s):
    page = pt_ref[seq, first_page + i]
    dst = buf_ref.at[slot, pl.ds((s * CP_PAGES + i) * HD, HD), :]
    return pltpu.make_async_copy(pool_ref.at[page, s], dst, sem.at[slot])


def _fetch_pages(pool_ref, pt_ref, buf_ref, sem, seq, first_page, slot, wait):
    for i in range(CP_PAGES):
        for s in range(2):
            cp = _slab_copy(pool_ref, pt_ref, buf_ref, sem, seq, first_page, slot, i, s)
            if wait:
                cp.wait()
            else:
                cp.start()


def _compress_pages_kernel(pt_ref, pool_ref, w1_hbm, pe_ref, w2_ref, kg_ref, kc_ref, vc_ref, buf_ref, w1_ref, sem, wsem):
    step = pl.program_id(0)
    n_steps = pl.num_programs(0)
    halves = pt_ref.shape[1] // CP_PAGES
    fetch = functools.partial(_fetch_pages, pool_ref, pt_ref, buf_ref, sem)

    @pl.when(step == 0)
    def _():
        cp = pltpu.make_async_copy(w1_hbm, w1_ref, wsem)
        cp.start()
        fetch(0, 0, 0, False)
        cp.wait()

    for slot in range(2):
        @pl.when(step % 2 == slot)
        def _():
            @pl.when(step + 1 < n_steps)
            def _():
                nxt = step + 1
                fetch(nxt // halves, (nxt % halves) * CP_PAGES, 1 - slot, False)

            fetch(step // halves, (step % halves) * CP_PAGES, slot, True)
            ones_bd = _head_pair_ones()
            for s, out_ref in ((0, kc_ref), (1, vc_ref)):
                acc = jnp.zeros((N_KV_HEADS * CP_PAGES, BLOCKS_PER_PAGE * CMP_HIDDEN), F32)
                for dp in range(HEAD_DIM // 2):
                    lanes = []
                    for d in (2 * dp, 2 * dp + 1):
                        rows = [buf_ref[slot, pl.ds(s * CP_PAGES * HD + h * HEAD_DIM + d, CP_PAGES, stride=HD), :]
                                for h in range(N_KV_HEADS)]
                        lanes.append(jnp.concatenate(rows, axis=0))
                    acc = acc + _dot(jnp.concatenate(lanes, axis=1).astype(BF16), w1_ref[s, dp])
                hdn = _gelu(acc + pe_ref[s:s + 1, :])
                out = _dot(hdn.astype(BF16), w2_ref[s])
                if s == 0:
                    tiles = []
                    for j in range(out.shape[1] // LANES):
                        part = out[:, j * LANES:(j + 1) * LANES]
                        hi, lo = _split_bf16(part * part)
                        ss = _dot(hi, ones_bd) + _dot(lo, ones_bd)
                        tiles.append(part * lax.rsqrt(ss * (1.0 / HEAD_DIM) + NORM_EPS) * kg_ref[...])
                    out = jnp.concatenate(tiles, axis=1)
                for h in range(N_KV_HEADS):
                    out_ref[0, h] = out[h * CP_PAGES:(h + 1) * CP_PAGES, :]


def _compress_pages(pool_t, page_table, w):
    n_seq, n_pages = page_table.shape
    halves = n_pages // CP_PAGES
    consts = (w['cmp_pe4'], w['cmp_w2bd4'], w['k_gain2'][0:1])
    out = pl.BlockSpec((1, N_KV_HEADS, CP_PAGES, BLOCKS_PER_PAGE * HEAD_DIM), lambda i, pt: (i // halves, 0, i % halves, 0))
    grid_spec = pltpu.PrefetchScalarGridSpec(
        num_scalar_prefetch=1,
        grid=(n_seq * halves,),
        in_specs=[pl.BlockSpec(memory_space=pl.ANY), pl.BlockSpec(memory_space=pl.ANY)]
        + [pl.BlockSpec(c.shape, functools.partial(lambda nd, i, pt: (0,) * nd, len(c.shape))) for c in consts],
        out_specs=[out, out],
        scratch_shapes=[pltpu.VMEM((2, 2 * CP_PAGES * HD, PAGE_SIZE), F32), pltpu.VMEM(w['cmp_w1pair'].shape, BF16),
                        pltpu.SemaphoreType.DMA((2,)), pltpu.SemaphoreType.DMA(())],
    )
    shape = jax.ShapeDtypeStruct((n_seq, N_KV_HEADS, n_pages, BLOCKS_PER_PAGE * HEAD_DIM), F32)
    return pl.pallas_call(
        _compress_pages_kernel,
        grid_spec=grid_spec,
        out_shape=[shape, shape],
        compiler_params=_params(("arbitrary",)),
        name="compress_pages",
    )(page_table, pool_t, w['cmp_w1pair'], *consts)


def _diag_heads(o):
    rows = STEP_ROWS // N_KV_HEADS
    parts = []
    for h in range(N_KV_HEADS):
        tile = o[h * rows:(h + 1) * rows, (h // 2) * LANES:(h // 2 + 1) * LANES]
        parts.append(pltpu.roll(tile, HEAD_DIM, 1) if h % 2 else tile)
    return jnp.concatenate(parts, axis=0)


def _topk_columns(score, score_new, idx, idx_new, k):
    sel = jnp.zeros(score.shape, F32)
    sel_new = jnp.zeros(score_new.shape, F32)
    big = jnp.float32(1.0e9)
    for _ in range(k):
        mx = jnp.maximum(jnp.max(score, axis=0, keepdims=True), score_new)
        first = jnp.minimum(jnp.min(jnp.where(score == mx, idx, big), axis=0, keepdims=True),
                            jnp.where(score_new == mx, idx_new, big))
        pick = idx == first
        pick_new = first == idx_new
        sel = jnp.where(pick, 1.0, sel)
        sel_new = jnp.where(pick_new, 1.0, sel_new)
        score = jnp.where(pick, LOWEST, score)
        score_new = jnp.where(pick_new, LOWEST, score_new)
    return sel, sel_new


def _nsa_step_kernel(pt_ref, q_ref, gate_ref, kc_ref, vc_ref, snew_ref, wnew_ref, wnewt_ref, win_ref, exp_ref, pool_ref,
                     x_ref, wout_ref, buf_ref, sem):
    seq = pl.program_id(0)
    n_seq = pl.num_programs(0)
    n_pages = pt_ref.shape[1]
    past_len = n_pages * PAGE_SIZE
    n_past_blk = past_len // SEL_BLOCK
    fetch = functools.partial(_fetch_pages, pool_ref, pt_ref, buf_ref, sem)

    @pl.when(seq == 0)
    def _():
        fetch(0, 0, 0, False)

    fetch(seq, CP_PAGES, 1, False)

    q = q_ref[0]
    row = lax.broadcasted_iota(jnp.int32, (STEP_ROWS, 1), 0)
    t_row = (row // GROUP) % DEC_T
    qpos = past_len + t_row
    lane = lax.broadcasted_iota(jnp.int32, (1, LANES), 1)
    lower = lane < HEAD_DIM
    rows_h = STEP_ROWS // N_KV_HEADS

    c_even = BLOCKS_PER_PAGE * (lane % PAGES_PER_SEQ) + 2 * (lane // PAGES_PER_SEQ)
    zero_q = jnp.zeros((STEP_ROWS, LANES), BF16)
    s_even = jnp.zeros((STEP_ROWS, LANES), F32)
    s_odd = jnp.zeros((STEP_ROWS, LANES), F32)
    kv_cat = []
    for h in range(N_KV_HEADS):
        tile = q[:, (h // 2) * LANES:(h // 2 + 1) * LANES]
        swapped = pltpu.roll(tile.astype(F32), HEAD_DIM, 1).astype(BF16)
        q_lo = jnp.where(lower, swapped if h % 2 else tile, zero_q)
        q_hi = jnp.where(lower, zero_q, tile if h % 2 else swapped)
        kcat = jnp.concatenate([kc_ref[0, h, :, 0:LANES], kc_ref[0, h, :, LANES:2 * LANES]], axis=0).astype(BF16)
        vcat = jnp.concatenate([vc_ref[0, h, :, 0:LANES], vc_ref[0, h, :, LANES:2 * LANES]], axis=0)
        kv_cat.append(vcat)
        s_even = s_even + _dot_nt(q_lo, kcat)
        s_odd = s_odd + _dot_nt(q_hi, kcat)
    m_even = (c_even + 1) * CMP_BLOCK - 1 <= qpos
    m_odd = (c_even + 2) * CMP_BLOCK - 1 <= qpos
    s_even = jnp.where(m_even, s_even, NEG_INF)
    s_odd = jnp.where(m_odd, s_odd, NEG_INF)
    mx = jnp.maximum(jnp.max(s_even, axis=-1, keepdims=True), jnp.max(s_odd, axis=-1, keepdims=True))
    e_even = jnp.where(m_even, jnp.exp(s_even - mx), 0.0)
    e_odd = jnp.where(m_odd, jnp.exp(s_odd - mx), 0.0)
    den = jnp.maximum(jnp.sum(e_even, axis=-1, keepdims=True) + jnp.sum(e_odd, axis=-1, keepdims=True), TINY)
    pr_even = e_even / den
    pr_odd = e_odd / den
    oc = jnp.zeros((STEP_ROWS, LANES), F32)
    for h in range(N_KV_HEADS):
        mine = (row // rows_h) == h
        oc = (oc + _dot(jnp.where(mine, pr_even, 0.0).astype(BF16), jnp.where(lower, kv_cat[h], 0.0).astype(BF16))
              + _dot(jnp.where(mine, pr_odd, 0.0).astype(BF16), jnp.where(lower, 0.0, kv_cat[h]).astype(BF16)))
    o_cmp = oc + pltpu.roll(oc, HEAD_DIM, 1)
    pr_sum = [pr_even + pr_odd]

    group_sum = jnp.where(lax.broadcasted_iota(jnp.int32, (STEP_ROWS, STEP_ROWS), 0) // GROUP
                          == lax.broadcasted_iota(jnp.int32, (STEP_ROWS, STEP_ROWS), 1) // GROUP, 1.0, 0.0).astype(BF16)
    p_hi, p_lo = _split_bf16(jnp.concatenate(pr_sum, axis=0))
    imp_t = _dot_tn(p_hi, group_sum) + _dot_tn(p_lo, group_sum)
    sub = lax.broadcasted_iota(jnp.int32, (LANES, 1), 0)
    blk = 2 * (sub % PAGES_PER_SEQ) + sub // PAGES_PER_SEQ
    col = lax.broadcasted_iota(jnp.int32, (1, STEP_ROWS), 1)
    qpos_c = past_len + (col // GROUP) % DEC_T
    cur = qpos_c // SEL_BLOCK
    forced = (blk == 0) | (blk == cur) | (blk == cur - 1)
    score = jnp.where(blk * SEL_BLOCK <= qpos_c, imp_t + jnp.where(forced, FORCE_BONUS, 0.0), NEG_INF)
    forced_new = (n_past_blk == cur) | (n_past_blk == cur - 1)
    score_new = jnp.where(n_past_blk * SEL_BLOCK <= qpos_c, jnp.where(forced_new, FORCE_BONUS, 0.0), NEG_INF)
    sel_t, sel_new_t = _topk_columns(score, score_new, blk.astype(F32), jnp.float32(n_past_blk), SEL_TOPK)
    bias_t = jnp.where(sel_t > 0.5, 0.0, NEG_INF).astype(BF16)
    flag_rows = jnp.where(lax.broadcasted_iota(jnp.int32, (NEW_ROWS, 1), 0) == 0, sel_new_t, 0.0).astype(BF16)
    sel_new = _dot_tn(flag_rows, jnp.ones((NEW_ROWS, LANES), BF16))[:, 0:1]

    def page_tiles(slot, page0, carry):
        m, l, acc = carry
        for sc in range(CP_PAGES // SUB_PAGES):
            pages = [sc * SUB_PAGES + i for i in range(0, SUB_PAGES, 2)]
            slab2 = lambda s, i: jnp.concatenate([buf_ref[slot, pl.ds((s * CP_PAGES + i) * HD, HD), :].astype(BF16),
                                                  buf_ref[slot, pl.ds((s * CP_PAGES + i + 1) * HD, HD), :].astype(BF16)], axis=1)
            s_t = jnp.concatenate([_dot(q, slab2(0, i)) for i in pages], axis=1)
            first_key = (page0 + sc * SUB_PAGES) * PAGE_SIZE
            s_t = s_t + _dot_tn(bias_t, exp_ref[:, first_key:first_key + SUB_PAGES * PAGE_SIZE])
            m_new = jnp.maximum(m, jnp.max(s_t, axis=-1, keepdims=True))
            alpha = jnp.exp(m - m_new)
            p = jnp.exp(s_t - m_new)
            l = alpha * l + jnp.sum(p, axis=-1, keepdims=True)
            acc = acc * alpha
            for k, i in enumerate(pages):
                acc = acc + _dot_nt(p[:, 2 * k * PAGE_SIZE:2 * (k + 1) * PAGE_SIZE].astype(BF16), slab2(1, i))
            m = m_new
        return m, l, acc

    carry = (jnp.full((STEP_ROWS, 1), NEG_INF, F32), jnp.zeros((STEP_ROWS, 1), F32), jnp.zeros((STEP_ROWS, HD), F32))
    fetch(seq, 0, 0, True)
    carry = page_tiles(0, 0, carry)

    @pl.when(seq + 1 < n_seq)
    def _():
        fetch(seq + 1, 0, 0, False)

    fetch(seq, CP_PAGES, 1, True)
    m, l, acc = page_tiles(1, CP_PAGES, carry)

    new_vis = lax.broadcasted_iota(jnp.int32, (1, NEW_ROWS), 1) <= t_row
    snew = snew_ref[0]
    s_n = jnp.where(new_vis & (sel_new > 0.5), _dot_nt(q, snew[:, 0:HD].astype(BF16)), NEG_INF)
    m_new = jnp.maximum(m, jnp.max(s_n, axis=-1, keepdims=True))
    alpha = jnp.exp(m - m_new)
    p_n = jnp.exp(s_n - m_new)
    l = alpha * l + jnp.sum(p_n, axis=-1, keepdims=True)
    acc = acc * alpha + _dot(p_n.astype(BF16), snew[:, HD:2 * HD].astype(BF16))
    o_sel = _diag_heads(acc / l)

    lb = win_ref.shape[3]
    kpos = past_len - lb + lax.broadcasted_iota(jnp.int32, (1, lb), 1)
    diff = qpos - kpos
    s_w = jnp.where((diff >= 0) & (diff < WINDOW) & (kpos >= 0), _dot(q, win_ref[0, 0].astype(BF16)), NEG_INF)
    wnew = wnew_ref[0]
    s_wn = jnp.where(new_vis, _dot_nt(q, wnew[:, 0:HD].astype(BF16)), NEG_INF)
    mw = jnp.maximum(jnp.max(s_w, axis=-1, keepdims=True), jnp.max(s_wn, axis=-1, keepdims=True))
    p_w = jnp.exp(s_w - mw)
    p_wn = jnp.exp(s_wn - mw)
    l_w = jnp.sum(p_w, axis=-1, keepdims=True) + jnp.sum(p_wn, axis=-1, keepdims=True)
    acc_w = _dot_nt(p_w.astype(BF16), win_ref[0, 1].astype(BF16)) + _dot(p_wn.astype(BF16), wnew[:, HD:2 * HD].astype(BF16))
    o_win = _diag_heads(acc_w / l_w)

    gate = gate_ref[0]
    x_ref[0] = gate[:, 0:1] * o_cmp + gate[:, 1:2] * o_sel + gate[:, 2:3] * o_win

    shift = LANES - DEC_T
    keep = lane < shift
    n_tiles = lb // LANES
    for s in range(2):
        for k in range(n_tiles):
            cur_tile = pltpu.roll(win_ref[0, s, :, k * LANES:(k + 1) * LANES], shift, 1)
            nxt_tile = (pltpu.roll(win_ref[0, s, :, (k + 1) * LANES:(k + 2) * LANES], shift, 1) if k + 1 < n_tiles
                        else wnewt_ref[0, s])
            wout_ref[0, s, :, k * LANES:(k + 1) * LANES] = jnp.where(keep, cur_tile, nxt_tile)


def _nsa_step(q_bd, gates, kc, vc, snew8, wnew8, wnew_t, win_t, expand, pool_t, page_table):
    n_seq, n_pages = page_table.shape
    lb = win_t.shape[3]
    per_seq = lambda shape: pl.BlockSpec((1,) + shape, functools.partial(lambda nd, i, pt: (i,) + (0,) * nd, len(shape)))
    cshape = (N_KV_HEADS, n_pages, BLOCKS_PER_PAGE * HEAD_DIM)
    grid_spec = pltpu.PrefetchScalarGridSpec(
        num_scalar_prefetch=1,
        grid=(n_seq,),
        in_specs=[per_seq((STEP_ROWS, HD)), per_seq((STEP_ROWS, LANES)), per_seq(cshape), per_seq(cshape),
                  per_seq((NEW_ROWS, KV_ROW)), per_seq((NEW_ROWS, KV_ROW)), per_seq((2, HD, LANES)), per_seq((2, HD, lb)),
                  pl.BlockSpec(expand.shape, lambda i, pt: (0, 0)), pl.BlockSpec(memory_space=pl.ANY)],
        out_specs=[per_seq((STEP_ROWS, LANES)), per_seq((2, HD, lb))],
        scratch_shapes=[pltpu.VMEM((2, 2 * CP_PAGES * HD, PAGE_SIZE), F32), pltpu.SemaphoreType.DMA((2,))],
    )
    return pl.pallas_call(
        _nsa_step_kernel,
        grid_spec=grid_spec,
        out_shape=[jax.ShapeDtypeStruct((n_seq, STEP_ROWS, LANES), F32), jax.ShapeDtypeStruct((n_seq, 2, HD, lb), F32)],
        compiler_params=_params(("arbitrary",)),
        name="nsa_step",
    )(page_table, q_bd, gates, kc, vc, snew8, wnew8, wnew_t, win_t, expand, pool_t)


def _step_query_layout(q_s, n_seq):
    q5 = q_s.reshape(n_seq, DEC_T, N_KV_HEADS, GROUP, HEAD_DIM)
    qt = jnp.transpose(q5, (0, 2, 1, 3, 4)).reshape(n_seq, N_KV_HEADS, DEC_T * GROUP, HEAD_DIM)
    eye = jnp.eye(N_KV_HEADS, dtype=q_s.dtype)
    return jnp.einsum('nhrd,hk->nhrkd', qt, eye).reshape(n_seq, STEP_ROWS, HD)


def _step_gate_layout(bgx_s, n_seq):
    bg = bgx_s.reshape(n_seq, DEC_T, 3, N_KV_HEADS, GROUP, HEAD_DIM)[..., 0]
    bg = jnp.transpose(bg, (0, 3, 1, 4, 2)).reshape(n_seq, STEP_ROWS, 3)
    return jnp.pad(bg, ((0, 0), (0, 0), (0, LANES - 3)))


def _step_output_layout(x, n_seq):
    x5 = x[:, :, :HEAD_DIM].reshape(n_seq, N_KV_HEADS, DEC_T, GROUP, HEAD_DIM)
    return jnp.transpose(x5, (0, 2, 1, 3, 4)).reshape(n_seq * DEC_T, C_Q)


def _block_expand_matrix(n_pages):
    rows = jnp.arange(LANES)
    keys = jnp.arange(n_pages * PAGE_SIZE)
    hit = (((rows[:, None] % PAGES_PER_SEQ) == (keys[None, :] // PAGE_SIZE))
           & ((rows[:, None] // PAGES_PER_SEQ) == ((keys[None, :] % PAGE_SIZE) // SEL_BLOCK)))
    return hit.astype(BF16)


PEER_HALF = PEER_KEY_DIM // 2
N_SIDES = 2 * PEER_HEADS


def _merge_kernel(x_ref, conv_ref, o_ref, gmix_ref, wmg_ref, wao_ref, wo_ref, gffn_ref, wq_ref, sk_ref,
                  h_ref, hn_ref, st_ref):
    x = x_ref[...]
    n = (x * lax.rsqrt(jnp.mean(x * x, axis=-1, keepdims=True) + NORM_EPS) * gmix_ref[...]).astype(BF16)
    mg = _sigmoid(_dot(n, wmg_ref[...]))
    attn = _dot(o_ref[...], wao_ref[...])
    mix = mg[:, :D_MODEL] * conv_ref[...] + mg[:, D_MODEL:] * attn
    h = x + _dot(mix.astype(BF16), wo_ref[...])
    h_ref[...] = h
    hn = (h * lax.rsqrt(jnp.mean(h * h, axis=-1, keepdims=True) + NORM_EPS) * gffn_ref[...]).astype(BF16)
    hn_ref[...] = hn
    qp = _dot(hn, wq_ref[...]).astype(BF16)
    for i in range(N_SIDES):
        st_ref[i * N_KEYS:(i + 1) * N_KEYS, :] = _dot_nt(sk_ref[i], qp[:, i * PEER_HALF:(i + 1) * PEER_HALF])


def _merge(x2d, conv_out, o, w, row_tile):
    ntok = x2d.shape[0]
    row = lambda width: pl.BlockSpec((row_tile, width), lambda i: (i, 0))
    consts = (w['g_mix'], w['w_mg'], w['w_attn_out'], w['w_o'], w['g_ffn'], w['peer_wq'], w['peer_subkeys'])
    return pl.pallas_call(
        _merge_kernel,
        grid=(ntok // row_tile,),
        in_specs=[row(D_MODEL), row(D_MODEL), row(C_Q)] + [_const_spec(c.shape) for c in consts],
        out_specs=[row(D_MODEL), row(D_MODEL), pl.BlockSpec((N_SIDES * N_KEYS, row_tile), lambda i: (0, i))],
        out_shape=[jax.ShapeDtypeStruct((ntok, D_MODEL), F32), jax.ShapeDtypeStruct((ntok, D_MODEL), BF16),
                   jax.ShapeDtypeStruct((N_SIDES * N_KEYS, ntok), F32)],
        compiler_params=_params(("parallel",)),
        name="merge",
    )(x2d, conv_out, o, *consts)


def _cand_groups():
    groups = []
    for j in range(PEER_TOPK):
        n_valid = PEER_TOPK // (j + 1)
        groups.append((j, n_valid, -(-n_valid // 8) * 8))
    return groups


def _top_rows_sorted(s, k):
    n = s.shape[0]
    row = lax.broadcasted_iota(jnp.int32, (n, 1), 0).astype(F32)
    out_row = lax.broadcasted_iota(jnp.int32, (k, 1), 0)
    vals = jnp.zeros((k, s.shape[1]), F32)
    idxs = jnp.zeros((k, s.shape[1]), F32)
    for r in range(k):
        mx = jnp.max(s, axis=0, keepdims=True)
        first = jnp.min(jnp.where(s == mx, row, float(n)), axis=0, keepdims=True)
        vals = jnp.where(out_row == r, mx, vals)
        idxs = jnp.where(out_row == r, first, idxs)
        s = jnp.where(row == first, LOWEST, s)
    return vals, idxs


def _peer_topk_kernel(st_ref, a_ref, b_ref, g_ref):
    tt = st_ref.shape[1]
    groups = _cand_groups()
    row16 = lax.broadcasted_iota(jnp.int32, (PEER_TOPK, 1), 0).astype(F32)
    a_rows, b_rows, g_rows = [], [], []
    for hd in range(PEER_HEADS):
        va, ia = _top_rows_sorted(st_ref[(2 * hd) * N_KEYS:(2 * hd + 1) * N_KEYS, :], PEER_TOPK)
        vb, ib = _top_rows_sorted(st_ref[(2 * hd + 1) * N_KEYS:(2 * hd + 2) * N_KEYS, :], PEER_TOPK)
        cands, flats = [], []
        for j, n_valid, n_rows in groups:
            i_idx = lax.broadcasted_iota(jnp.int32, (n_rows, 1), 0)
            cands.append(jnp.where(i_idx < n_valid, va[0:n_rows, :] + vb[j:j + 1, :], LOWEST))
            flats.append((i_idx * PEER_TOPK + j).astype(F32))
        cand = jnp.concatenate(cands, axis=0)
        flat = jnp.concatenate(flats, axis=0)
        n_flat = float(PEER_TOPK * PEER_TOPK)
        sc = jnp.zeros((PEER_TOPK, tt), F32)
        ea = jnp.zeros((PEER_TOPK, tt), F32)
        eb = jnp.zeros((PEER_TOPK, tt), F32)
        for r in range(PEER_TOPK):
            mx = jnp.max(cand, axis=0, keepdims=True)
            first = jnp.min(jnp.where(cand == mx, flat, n_flat), axis=0, keepdims=True)
            cand = jnp.where(flat == first, LOWEST, cand)
            i_sel = jnp.floor(first * (1.0 / PEER_TOPK))
            j_sel = first - i_sel * PEER_TOPK
            a_sel = jnp.sum(jnp.where(row16 == i_sel, ia, 0.0), axis=0, keepdims=True)
            b_sel = jnp.sum(jnp.where(row16 == j_sel, ib, 0.0), axis=0, keepdims=True)
            sc = jnp.where(row16 == r, mx, sc)
            ea = jnp.where(row16 == r, a_sel, ea)
            eb = jnp.where(row16 == r, b_sel, eb)
        e = jnp.exp(sc - sc[0:1, :])
        g_rows.append(e / jnp.sum(e, axis=0, keepdims=True))
        a_rows.append(ea)
        b_rows.append(eb)
    a_ref[...] = jnp.transpose(jnp.concatenate(a_rows, axis=0))
    b_ref[...] = jnp.transpose(jnp.concatenate(b_rows, axis=0))
    g_ref[...] = jnp.transpose(jnp.concatenate(g_rows, axis=0))


def _peer_topk(scores_t, tok_tile):
    ntok = scores_t.shape[1]
    n_pick = PEER_HEADS * PEER_TOPK
    out = pl.BlockSpec((tok_tile, n_pick), lambda i: (i, 0))
    return pl.pallas_call(
        _peer_topk_kernel,
        grid=(ntok // tok_tile,),
        in_specs=[pl.BlockSpec((N_SIDES * N_KEYS, tok_tile), lambda i: (0, i))],
        out_specs=[out, out, out],
        out_shape=[jax.ShapeDtypeStruct((ntok, n_pick), F32)] * 3,
        compiler_params=_params(("parallel",)),
        name="peer_topk",
    )(scores_t)


W_TOK = 64


def _peer_w_kernel(a_ref, b_ref, g_ref, w_ref):
    sub = lax.broadcasted_iota(jnp.int32, (N_KEYS, 1), 0).astype(F32)

    def body(t, carry):
        a_row = a_ref[pl.ds(t, 1), :]
        b_row = b_ref[pl.ds(t, 1), :]
        g_row = g_ref[pl.ds(t, 1), :]
        ga = jnp.where(sub == a_row, g_row, 0.0).astype(BF16)
        ob = jnp.where(sub == b_row, 1.0, 0.0).astype(BF16)
        w_ref[t] = _dot_nt(ga, ob)
        return carry

    lax.fori_loop(0, W_TOK, body, 0, unroll=4)


def _peer_weights(a_idx, b_idx, gate):
    ntok, n_pick = a_idx.shape
    row = pl.BlockSpec((W_TOK, n_pick), lambda i: (i, 0))
    return pl.pallas_call(
        _peer_w_kernel,
        grid=(ntok // W_TOK,),
        in_specs=[row, row, row],
        out_specs=pl.BlockSpec((W_TOK, N_KEYS, N_KEYS), lambda i: (i, 0, 0)),
        out_shape=jax.ShapeDtypeStruct((ntok, N_KEYS, N_KEYS), F32),
        compiler_params=_params(("parallel",)),
        name="peer_weights",
    )(a_idx, b_idx, gate)


PEER_TOK = 512
PEER_ABLK = 16


def _peer_dense_kernel(hn_ref, h_ref, w_ref, u_ref, v_ref, y_ref, acc_ref):
    j = pl.program_id(1)

    @pl.when(j == 0)
    def _():
        acc_ref[...] = jnp.zeros(acc_ref.shape, F32)

    hn = hn_ref[...]
    pair = 2 * N_KEYS
    for k in range(PEER_ABLK // 2):
        act = _dot_nt(hn, u_ref[k * pair:(k + 1) * pair, :])
        wk = jnp.concatenate([w_ref[:, 2 * k, :], w_ref[:, 2 * k + 1, :]], axis=1)
        acc_ref[...] += _dot((wk * _gelu(act)).astype(BF16), v_ref[k * pair:(k + 1) * pair, :])

    @pl.when(j == pl.num_programs(1) - 1)
    def _():
        y_ref[...] = h_ref[...] + acc_ref[...]


def _peer_dense(hn, h, w_atb, u, v):
    ntok = hn.shape[0]
    blk = PEER_ABLK * N_KEYS
    tok = lambda: pl.BlockSpec((PEER_TOK, D_MODEL), lambda i, j: (i, 0))
    exp = lambda: pl.BlockSpec((blk, D_MODEL), lambda i, j: (j, 0))
    return pl.pallas_call(
        _peer_dense_kernel,
        grid=(ntok // PEER_TOK, N_KEYS // PEER_ABLK),
        in_specs=[tok(), tok(), pl.BlockSpec((PEER_TOK, PEER_ABLK, N_KEYS), lambda i, j: (i, j, 0)), exp(), exp()],
        out_specs=tok(),
        out_shape=jax.ShapeDtypeStruct((ntok, D_MODEL), F32),
        scratch_shapes=[pltpu.VMEM((PEER_TOK, D_MODEL), F32)],
        compiler_params=_params(("parallel", "arbitrary")),
        name="peer_dense",
    )(hn, h, w_atb, u, v)


def kernel(x_prompt, x_sample, cache_cmp_kv, cache_sel_kv, state_win_kv, state_conv, page_table, g_mix, w_in, conv_w, conv_b, conv_ln_g, conv_ln_b, w_conv_out, q_gain, k_gain, cmp_pe, cmp_w1, cmp_w2, w_attn_out, w_o, g_ffn, peer_wq, peer_subkeys, peer_u, peer_v):
    layer = 0
    p = {'g_mix': g_mix, 'w_in': w_in, 'conv_w': conv_w, 'conv_b': conv_b, 'conv_ln_g': conv_ln_g,
         'conv_ln_b': conv_ln_b, 'w_conv_out': w_conv_out, 'q_gain': q_gain, 'k_gain': k_gain, 'cmp_pe': cmp_pe,
         'cmp_w1': cmp_w1, 'cmp_w2': cmp_w2, 'w_attn_out': w_attn_out, 'w_o': w_o, 'g_ffn': g_ffn,
         'peer_wq': peer_wq, 'peer_subkeys': peer_subkeys}
    w = _prep_weights({k: v[layer] for k, v in p.items()})
    u_bf = peer_u[layer].astype(BF16)
    v_bf = peer_v[layer].astype(BF16)

    def ffn(x2d, conv_out, o):
        h, hn, scores_t = _merge(x2d, conv_out, o, w, ROW_TILE)
        a_idx, b_idx, gate = _peer_topk(scores_t, ROW_TILE)
        return _peer_dense(hn, h, _peer_weights(a_idx, b_idx, gate), u_bf, v_bf)

    n, t, _ = x_prompt.shape
    xp = x_prompt.reshape(n * t, D_MODEL)
    cos_p, sin_p = _rope_tables(jnp.arange(t, dtype=jnp.int32))
    glu, q, ckv, skv, wkv, bgx, kds, vds, kdw, vdw = _project(xp, cos_p, sin_p, w, ROW_TILE)
    glu3 = glu.reshape(n, t, D_CONV)
    conv_p = _conv_prompt(glu3, w, ROW_TILE).reshape(n * t, D_MODEL)
    kc, vc = _compress_prompt(ckv, w, 128)
    o_p = _nsa_prompt(q, kc, vc, kds, vds, kdw, vdw, bgx, n, t)
    y_p = ffn(xp, conv_p, o_p).reshape(n, t, D_MODEL)

    n_s, t_s, _ = x_sample.shape
    past_len = page_table.shape[1] * PAGE_SIZE
    xs = x_sample.reshape(n_s * t_s, D_MODEL)
    pos_s = past_len + jnp.arange(t_s, dtype=jnp.int32)
    cos_s, sin_s = _rope_tables(jnp.tile(pos_s, ROW_TILE // t_s))
    glu_s, q_s, ckv_s, skv_s, wkv_s, bgx_s = _project(xs, cos_s, sin_s, w, ROW_TILE)[:6]
    glu_s3 = glu_s.reshape(n_s, t_s, D_CONV)
    conv_s = _conv_step(jnp.swapaxes(state_conv[layer], 0, 1), jnp.swapaxes(glu_s3, 0, 1), w)
    conv_s = jnp.swapaxes(conv_s, 0, 1).reshape(n_s * t_s, D_MODEL)
    assert t_s == DEC_T and page_table.shape[1] == PAGES_PER_SEQ
    kc_s, vc_s = _compress_pages(_native_pool(cache_cmp_kv[layer]), page_table, w)
    lb = state_win_kv.shape[2]
    win_t = jnp.transpose(state_win_kv[layer], (0, 2, 3, 4, 1)).reshape(n_s, 2, HD, lb)
    pad_new = lambda a: jnp.pad(a.reshape(n_s, t_s, KV_ROW), ((0, 0), (0, NEW_ROWS - t_s), (0, 0)))
    wnew_t = jnp.pad(jnp.transpose(wkv_s.reshape(n_s, t_s, 2, HD), (0, 2, 3, 1)), ((0, 0), (0, 0), (0, 0), (LANES - t_s, 0)))
    x_step, win_s = _nsa_step(_step_query_layout(q_s, n_s), _step_gate_layout(bgx_s, n_s), kc_s, vc_s,
                              pad_new(skv_s), pad_new(wkv_s), wnew_t, win_t, _block_expand_matrix(PAGES_PER_SEQ),
                              _native_pool(cache_sel_kv[layer]), page_table)
    o_s = _step_output_layout(x_step, n_s).astype(BF16)
    y_s = ffn(xs, conv_s, o_s).reshape(n_s, t_s, D_MODEL)
    win_s = jnp.transpose(win_s.reshape(n_s, 2, N_KV_HEADS, HEAD_DIM, lb), (0, 4, 1, 2, 3))

    kv6 = lambda a, nn, tt: a.reshape(1, nn, tt, 2, N_KV_HEADS, HEAD_DIM)
    n_win = min(WINDOW, t)
    new_conv_s = jnp.concatenate([state_conv[layer][:, t_s:], glu_s3], axis=1)
    return (y_p, y_s,
            kv6(ckv, n, t), kv6(skv, n, t), kv6(wkv, n, t)[:, :, t - n_win:], glu3[None, :, t - (CONV_WIDTH - 1):],
            kv6(ckv_s, n_s, t_s), kv6(skv_s, n_s, t_s), win_s[None], new_conv_s[None])
```

```python
import functools
import math

import jax
import jax.numpy as jnp
from jax import lax
from jax.experimental import pallas as pl
from jax.experimental.pallas import tpu as pltpu

F32 = jnp.float32
BF16 = jnp.bfloat16

D_MODEL = 1024
D_CONV = 512
CONV_WIDTH = 31
N_HEADS = 16
N_KV_HEADS = 4
HEAD_DIM = 64
GROUP = N_HEADS // N_KV_HEADS
CMP_BLOCK = 32
CMP_HIDDEN = 128
SEL_BLOCK = 64
SEL_TOPK = 16
WINDOW = 512
ROPE_THETA = 10000.0
FORCE_BONUS = 1.0e3
N_KEYS = 128
PEER_HEADS = 8
PEER_TOPK = 16
PEER_KEY_DIM = 256
NORM_EPS = 1e-6
NEG_INF = -1.0e30
TINY = 1.0e-30
ATTN_SCALE = HEAD_DIM ** -0.5
KV_ROW = 2 * N_KV_HEADS * HEAD_DIM
C_Q = N_HEADS * HEAD_DIM

ROW_TILE = 256
LANES = 128
VMEM_LIMIT = 56 * 1024 * 1024


def _dot(a, b):
    return jnp.dot(a, b, preferred_element_type=F32)


def _dot_nt(a, b):
    return lax.dot_general(a, b, (((1,), (1,)), ((), ())), preferred_element_type=F32)


def _dot_tn(a, b):
    return lax.dot_general(a, b, (((0,), (0,)), ((), ())), preferred_element_type=F32)


def _split_bf16(x):
    hi = x.astype(BF16)
    lo = (x - hi.astype(F32)).astype(BF16)
    return hi, lo


def _sigmoid(x):
    return 1.0 / (1.0 + jnp.exp(-x))


def _params(sem):
    return pltpu.CompilerParams(dimension_semantics=sem, vmem_limit_bytes=VMEM_LIMIT)


def _const_spec(shape):
    nd = len(shape)
    return pl.BlockSpec(shape, lambda *_: (0,) * nd)


def _head_pair_ones():
    r = lax.broadcasted_iota(jnp.int32, (LANES, LANES), 0) // HEAD_DIM
    c = lax.broadcasted_iota(jnp.int32, (LANES, LANES), 1) // HEAD_DIM
    return jnp.where(r == c, 1.0, 0.0).astype(BF16)


def _head_rms_rope(z, gain, cos, sin_signed, ones_bd, first_half):
    hi, lo = _split_bf16(z * z)
    ss = _dot(hi, ones_bd) + _dot(lo, ones_bd)
    zn = z * lax.rsqrt(ss * (1.0 / HEAD_DIM) + NORM_EPS) * gain
    partner = jnp.where(first_half, pltpu.roll(zn, LANES - HEAD_DIM // 2, 1), pltpu.roll(zn, HEAD_DIM // 2, 1))
    return zn * cos + partner * sin_signed


def _proj_kernel(x_ref, cos_ref, sin_ref, gmix_ref, wglu_ref, wq_ref, wkv_ref, wbg_ref, qg_ref, kg_ref,
                 glu_ref, q_ref, ckv_ref, skv_ref, wkvo_ref, bg_ref, ksd_ref, vsd_ref, kwd_ref, vwd_ref,
                 ckvt_ref, skvt_ref, wkvt_ref):
    dup_refs = (ksd_ref, vsd_ref, kwd_ref, vwd_ref)
    x = x_ref[...]
    ms = jnp.mean(x * x, axis=-1, keepdims=True)
    n = (x * lax.rsqrt(ms + NORM_EPS) * gmix_ref[...]).astype(BF16)
    cos = cos_ref[...]
    sin_signed = sin_ref[...]
    ones_bd = _head_pair_ones()
    lane = lax.broadcasted_iota(jnp.int32, (1, LANES), 1)
    first_half = (lane % HEAD_DIM) < (HEAD_DIM // 2)

    zg = _dot(n, wglu_ref[...])
    glu_ref[...] = zg[:, :D_CONV] * _sigmoid(zg[:, D_CONV:])

    zq = _dot(n, wq_ref[...])
    qg = qg_ref[...]
    for j in range(C_Q // LANES):
        sl = slice(j * LANES, (j + 1) * LANES)
        qj = _head_rms_rope(zq[:, sl], qg, cos, sin_signed, ones_bd, first_half)
        q_ref[:, sl] = (qj * ATTN_SCALE).astype(BF16)

    zkv = _dot(n, wkv_ref[...])
    half = KV_ROW // 2
    lower = lane < HEAD_DIM
    for b, (out_ref, t_ref) in enumerate(((ckv_ref, ckvt_ref), (skv_ref, skvt_ref), (wkvo_ref, wkvt_ref))):
        kg = kg_ref[b:b + 1, :]
        t_ref[0, half:, :] = jnp.transpose(zkv[:, b * KV_ROW + half:(b + 1) * KV_ROW])
        for j in range(half // LANES):
            sl = slice(b * KV_ROW + j * LANES, b * KV_ROW + (j + 1) * LANES)
            kj = _head_rms_rope(zkv[:, sl], kg, cos, sin_signed, ones_bd, first_half)
            out_ref[:, j * LANES:(j + 1) * LANES] = kj
            t_ref[0, j * LANES:(j + 1) * LANES, :] = jnp.transpose(kj)
            vj = zkv[:, b * KV_ROW + half + j * LANES:b * KV_ROW + half + (j + 1) * LANES]
            if b > 0:
                for src, dup_ref in ((kj, dup_refs[2 * (b - 1)]), (vj, dup_refs[2 * (b - 1) + 1])):
                    swapped = pltpu.roll(src, HEAD_DIM, 1)
                    dup_ref[:, 2 * j * LANES:(2 * j + 1) * LANES] = jnp.where(lower, src, swapped).astype(BF16)
                    dup_ref[:, (2 * j + 1) * LANES:(2 * j + 2) * LANES] = jnp.where(lower, swapped, src).astype(BF16)
        out_ref[:, half:] = zkv[:, b * KV_ROW + half:(b + 1) * KV_ROW]

    bg_ref[...] = _sigmoid(_dot(n, wbg_ref[...]))


def _rope_tables(pos):
    half = HEAD_DIM // 2
    inv = jnp.exp(-(2.0 * math.log(ROPE_THETA) / HEAD_DIM) * jnp.arange(half, dtype=F32))
    ang = pos.astype(F32)[:, None] * inv[None, :]
    lane = jnp.arange(LANES)
    cos = jnp.cos(ang)[:, lane % half]
    sin = jnp.sin(ang)[:, lane % half]
    sin_signed = jnp.where((lane % HEAD_DIM) < half, -sin, sin)
    return cos, sin_signed


def _project(x2d, cos, sin_signed, w, row_tile, t_seq):
    ntok = x2d.shape[0]
    n_pos_blocks = cos.shape[0] // row_tile
    nb = t_seq // row_tile
    row = lambda width: pl.BlockSpec((row_tile, width), lambda i: (i, 0))
    tab = pl.BlockSpec((row_tile, LANES), lambda i: (i % n_pos_blocks, 0))
    tr = pl.BlockSpec((1, KV_ROW, row_tile), lambda i: (i // nb, 0, i % nb))
    consts = (w['g_mix'], w['w_glu'], w['w_q'], w['w_kv'], w['w_bgx'], w['q_gain2'], w['k_gain2'])
    return pl.pallas_call(
        _proj_kernel,
        grid=(ntok // row_tile,),
        in_specs=[row(D_MODEL), tab, tab] + [_const_spec(c.shape) for c in consts],
        out_specs=[row(D_CONV), row(C_Q), row(KV_ROW), row(KV_ROW), row(KV_ROW), row(3 * C_Q)] + [row(KV_ROW)] * 4 + [tr] * 3,
        out_shape=[jax.ShapeDtypeStruct((ntok, D_CONV), F32), jax.ShapeDtypeStruct((ntok, C_Q), BF16),
                   jax.ShapeDtypeStruct((ntok, KV_ROW), F32), jax.ShapeDtypeStruct((ntok, KV_ROW), F32),
                   jax.ShapeDtypeStruct((ntok, KV_ROW), F32), jax.ShapeDtypeStruct((ntok, 3 * C_Q), F32)]
        + [jax.ShapeDtypeStruct((ntok, KV_ROW), BF16)] * 4
        + [jax.ShapeDtypeStruct((ntok // t_seq, KV_ROW, t_seq), F32)] * 3,
        compiler_params=_params(("parallel",)),
        name="proj",
    )(x2d, cos, sin_signed, *consts)


def _prep_weights(p):
    w_in = p['w_in']
    c_glu = 2 * D_CONV
    c_kv = 3 * KV_ROW
    c_bg = 3 * N_HEADS
    o1, o2, o3 = c_glu, c_glu + C_Q, c_glu + C_Q + c_kv
    o4 = o3 + c_bg
    row = lambda v: v.reshape(1, -1)
    return {
        'g_mix': row(p['g_mix']),
        'w_glu': w_in[:, :o1].astype(BF16),
        'w_q': w_in[:, o1:o2].astype(BF16),
        'w_kv': w_in[:, o2:o3].astype(BF16),
        'w_bgx': jnp.repeat(w_in[:, o3:o4], HEAD_DIM, axis=1).astype(BF16),
        'w_mg': w_in[:, o4:].astype(BF16),
        'q_gain2': row(jnp.tile(p['q_gain'], 2)),
        'k_gain2': jnp.tile(p['k_gain'], (1, 2)),
        'conv_w': p['conv_w'],
        'conv_b': row(p['conv_b']),
        'conv_ln_g': row(p['conv_ln_g']),
        'conv_ln_b': row(p['conv_ln_b']),
        'w_conv_out': p['w_conv_out'].astype(BF16),
        'cmp_w1bd': _pair_block_diag(p['cmp_w1']).astype(BF16),
        'cmp_w2bd': _pair_block_diag(p['cmp_w2']).astype(BF16),
        'cmp_w1pair': jnp.einsum('scde,bk->sdbcke', p['cmp_w1'], jnp.eye(4, dtype=F32)).reshape(
            2, HEAD_DIM // 2, 2 * PAGE_SIZE, 4 * CMP_HIDDEN).astype(BF16),
        'cmp_w2bd4': jnp.einsum('sed,bk->sbekd', p['cmp_w2'], jnp.eye(4, dtype=F32)).reshape(
            2, 4 * CMP_HIDDEN, 4 * HEAD_DIM).astype(BF16),
        'cmp_pe4': jnp.tile(jnp.einsum('scd,scde->se', p['cmp_pe'], p['cmp_w1'], precision=lax.Precision.HIGHEST), (1, 4)),
        'w_attn_out': p['w_attn_out'].astype(BF16),
        'w_o': p['w_o'].astype(BF16),
        'g_ffn': row(p['g_ffn']),
        'peer_wq': p['peer_wq'].astype(BF16),
        'peer_subkeys': p['peer_subkeys'].reshape(2 * PEER_HEADS, N_KEYS, PEER_KEY_DIM // 2).astype(BF16),
        'cmp_pe2': jnp.tile(jnp.einsum('scd,scde->se', p['cmp_pe'], p['cmp_w1'], precision=lax.Precision.HIGHEST), (1, 2)),
    }


def _pair_block_diag(a):
    z = jnp.zeros_like(a)
    return jnp.concatenate([jnp.concatenate([a, z], axis=-1), jnp.concatenate([z, a], axis=-1)], axis=-2)


CONV_HALO = 32
CONV_CHUNK = 64


def _ln_silu_project(y, lng, lnb, wout):
    mu = jnp.mean(y, axis=-1, keepdims=True)
    yc = y - mu
    var = jnp.mean(yc * yc, axis=-1, keepdims=True)
    yn = yc * lax.rsqrt(var + NORM_EPS) * lng + lnb
    act = yn * _sigmoid(yn)
    return _dot(act.astype(BF16), wout)


def _conv_prompt_kernel(glu_ref, cw_ref, cb_ref, lng_ref, lnb_ref, wout_ref, out_ref, ext_ref, y_ref):
    t = pl.program_id(1)
    tt = glu_ref.shape[1]

    @pl.when(t == 0)
    def _():
        ext_ref[0:CONV_HALO, :] = jnp.zeros((CONV_HALO, D_CONV), F32)

    @pl.when(t > 0)
    def _():
        ext_ref[0:CONV_HALO, :] = ext_ref[tt:tt + CONV_HALO, :]

    ext_ref[CONV_HALO:CONV_HALO + tt, :] = glu_ref[0]
    off = CONV_HALO - (CONV_WIDTH - 1)
    for r0 in range(0, tt, CONV_CHUNK):
        acc = jnp.zeros((CONV_CHUNK, D_CONV), F32)
        for j in range(CONV_WIDTH):
            acc = acc + ext_ref[r0 + off + j:r0 + off + j + CONV_CHUNK, :] * cw_ref[j:j + 1, :]
        y_ref[r0:r0 + CONV_CHUNK, :] = acc + cb_ref[...]
    out_ref[0] = _ln_silu_project(y_ref[...], lng_ref[...], lnb_ref[...], wout_ref[...])


def _conv_prompt(glu, w, t_tile):
    n, t, _ = glu.shape
    consts = (w['conv_w'], w['conv_b'], w['conv_ln_g'], w['conv_ln_b'], w['w_conv_out'])
    return pl.pallas_call(
        _conv_prompt_kernel,
        grid=(n, t // t_tile),
        in_specs=[pl.BlockSpec((1, t_tile, D_CONV), lambda i, j: (i, j, 0))] + [_const_spec(c.shape) for c in consts],
        out_specs=pl.BlockSpec((1, t_tile, D_MODEL), lambda i, j: (i, j, 0)),
        out_shape=jax.ShapeDtypeStruct((n, t, D_MODEL), F32),
        scratch_shapes=[pltpu.VMEM((t_tile + CONV_HALO, D_CONV), F32), pltpu.VMEM((t_tile, D_CONV), F32)],
        compiler_params=_params(("parallel", "arbitrary")),
        name="conv_prompt",
    )(glu, *consts)


def _conv_step_kernel(st_ref, gl_ref, cw_ref, cb_ref, lng_ref, lnb_ref, wout_ref, out_ref):
    n_buf = st_ref.shape[0]
    for t in range(gl_ref.shape[0]):
        acc = jnp.zeros(gl_ref.shape[1:], F32)
        for j in range(CONV_WIDTH):
            k = t + j
            row = st_ref[k] if k < n_buf else gl_ref[k - n_buf]
            acc = acc + row * cw_ref[j:j + 1, :]
        out_ref[t] = _ln_silu_project(acc + cb_ref[...], lng_ref[...], lnb_ref[...], wout_ref[...])


def _conv_step(st_tm, gl_tm, w):
    t, b, _ = gl_tm.shape
    consts = (w['conv_w'], w['conv_b'], w['conv_ln_g'], w['conv_ln_b'], w['w_conv_out'])
    return pl.pallas_call(
        _conv_step_kernel,
        grid=(1,),
        in_specs=[_const_spec(st_tm.shape), _const_spec(gl_tm.shape)] + [_const_spec(c.shape) for c in consts],
        out_specs=_const_spec((t, b, D_MODEL)),
        out_shape=jax.ShapeDtypeStruct((t, b, D_MODEL), F32),
        compiler_params=_params(("arbitrary",)),
        name="conv_step",
    )(st_tm, gl_tm, *consts)


BLOCK_COLS = CMP_BLOCK * KV_ROW
PAGE_SIZE = 128
PAGE_BLOCKS = PAGE_SIZE // CMP_BLOCK


def _gelu(x):
    return 0.5 * x * (1.0 + lax.erf(x * (2.0 ** -0.5)))


def _compress_rows(x_ref, m, w1_ref, pe_ref, w2_ref, kg_ref):
    outs = []
    for s in range(2):
        parts = []
        for hp in range(N_KV_HEADS // 2):
            acc = jnp.zeros((m, 2 * CMP_HIDDEN), F32)
            for c in range(CMP_BLOCK):
                col = c * KV_ROW + s * (KV_ROW // 2) + hp * LANES
                xs = x_ref[:, pl.ds(col, LANES)]
                acc = acc + _dot(xs.astype(BF16), w1_ref[s, c])
            hdn = _gelu(acc + pe_ref[s:s + 1, :])
            parts.append(_dot(hdn.astype(BF16), w2_ref[s]))
        outs.append(parts)
    ones_bd = _head_pair_ones()
    kc = []
    for part in outs[0]:
        hi, lo = _split_bf16(part * part)
        ss = _dot(hi, ones_bd) + _dot(lo, ones_bd)
        kc.append(part * lax.rsqrt(ss * (1.0 / HEAD_DIM) + NORM_EPS) * kg_ref[...])
    return jnp.concatenate(kc, axis=1), jnp.concatenate(outs[1], axis=1)


def _compress_prompt_kernel(x_ref, w1_ref, pe_ref, w2_ref, kg_ref, kc_ref, vc_ref):
    kc, vc = _compress_rows(x_ref, kc_ref.shape[0], w1_ref, pe_ref, w2_ref, kg_ref)
    kc_ref[...] = kc
    vc_ref[...] = vc


def _compress_prompt(ckv2d, w, m_tile):
    nblk = ckv2d.shape[0] // CMP_BLOCK
    x = ckv2d.reshape(nblk, BLOCK_COLS)
    consts = (w['cmp_w1bd'], w['cmp_pe2'], w['cmp_w2bd'], w['k_gain2'][0:1])
    out = pl.BlockSpec((m_tile, KV_ROW // 2), lambda i: (i, 0))
    return pl.pallas_call(
        _compress_prompt_kernel,
        grid=(nblk // m_tile,),
        in_specs=[pl.BlockSpec((m_tile, BLOCK_COLS), lambda i: (i, 0))] + [_const_spec(c.shape) for c in consts],
        out_specs=[out, out],
        out_shape=[jax.ShapeDtypeStruct((nblk, KV_ROW // 2), F32)] * 2,
        compiler_params=_params(("parallel",)),
        name="compress_prompt",
    )(x, *consts)


NSA_TQ = 256
CMP_PER_SEL = SEL_BLOCK // CMP_BLOCK
SEL_PER_TILE = NSA_TQ // SEL_BLOCK
WIN_TILES = WINDOW // NSA_TQ + 1
FLASH_ROWS = 256


def _dup_halves(x, in_head_half):
    return jnp.where(in_head_half, x, pltpu.roll(x, HEAD_DIM, 1))


def _rank_select(score, n_rows):
    row = lax.broadcasted_iota(jnp.int32, (n_rows, 1), 0)
    rank = jnp.zeros(score.shape, F32)
    for b in range(n_rows):
        sb = score[b:b + 1, :]
        tie_wins = jnp.where(row > b, 1.0, 0.0)
        rank = rank + jnp.where(sb > score, 1.0, jnp.where(sb == score, tie_wins, 0.0))
    return jnp.where(rank < SEL_TOPK, 1.0, 0.0)


def _flash_tile(q4, kd, vd, bias, m, l, acc):
    tq = bias.shape[0]
    lower = lax.broadcasted_iota(jnp.int32, (1, LANES), 1) < HEAD_DIM
    one = jnp.ones(vd.shape, BF16)
    v2 = jnp.concatenate([jnp.where(lower, vd, one), jnp.where(lower, one, vd)], axis=1)
    s = _dot_nt(q4, kd).reshape(GROUP, tq, vd.shape[0]) + bias[None]
    m_new = jnp.maximum(m, jnp.max(s, axis=-1, keepdims=True))
    alpha = jnp.exp(m - m_new)
    p = jnp.exp(s - m_new).astype(BF16).reshape(GROUP * tq, vd.shape[0])
    r = _dot(p, v2).reshape(GROUP, tq, 2 * LANES)
    sums = jnp.stack([r[g, :, HEAD_DIM:HEAD_DIM + 1] if g % 2 == 0 else r[g, :, LANES:LANES + 1] for g in range(GROUP)])
    new_acc = [acc[j] * jnp.where(lower, alpha[2 * j], alpha[2 * j + 1])
               + jnp.where(lower, r[2 * j, :, 0:LANES], r[2 * j + 1, :, LANES:2 * LANES]) for j in range(GROUP // 2)]
    return m_new, alpha * l + sums, new_acc


def _nsa_prompt_kernel(q_ref, kc_ref, vc_ref, ks_ref, vs_ref, kw0_ref, kw1_ref, kw2_ref, vw0_ref, vw1_ref, vw2_ref,
                       bg0_ref, bg1_ref, bg2_ref, o_ref):
    qi = pl.program_id(1)
    h = pl.program_id(2)
    tq = q_ref.shape[0]
    lane = lax.broadcasted_iota(jnp.int32, (1, LANES), 1)
    lower = lane < HEAD_DIM
    upper = lane >= HEAD_DIM
    in_head_half = (lane // HEAD_DIM) == (h % 2)
    pos = qi * tq + lax.broadcasted_iota(jnp.int32, (tq, 1), 0)

    q = q_ref[...]
    zero = jnp.zeros((tq, LANES), BF16)
    qm = [jnp.where(lower if g % 2 == 0 else upper, q[:, (g // 2) * LANES:(g // 2 + 1) * LANES], zero)
          for g in range(GROUP)]

    kcd = _dup_halves(kc_ref[...], in_head_half).astype(BF16)
    vcd = _dup_halves(vc_ref[...], in_head_half)
    vc_lo = jnp.where(lower, vcd, 0.0).astype(BF16)
    vc_hi = jnp.where(lower, 0.0, vcd).astype(BF16)
    nc = kc_ref.shape[0]
    blk_end = (lax.broadcasted_iota(jnp.int32, (1, nc), 1) + 1) * CMP_BLOCK - 1
    cmask = blk_end <= pos
    imp = jnp.zeros((tq, nc), F32)
    o_cmp = [jnp.zeros((tq, LANES), F32) for _ in range(GROUP // 2)]
    for g in range(GROUP):
        s = jnp.where(cmask, _dot_nt(qm[g], kcd), NEG_INF)
        e = jnp.where(cmask, jnp.exp(s - jnp.max(s, axis=-1, keepdims=True)), 0.0)
        pr = e / jnp.maximum(jnp.sum(e, axis=-1, keepdims=True), TINY)
        imp = imp + pr
        o_cmp[g // 2] = o_cmp[g // 2] + _dot(pr.astype(BF16), vc_lo if g % 2 == 0 else vc_hi)

    n_sel = nc // CMP_PER_SEL
    pair_t = jnp.where(lax.broadcasted_iota(jnp.int32, (n_sel, nc), 1) // CMP_PER_SEL
                       == lax.broadcasted_iota(jnp.int32, (n_sel, nc), 0), 1.0, 0.0).astype(BF16)
    imp_hi, imp_lo = _split_bf16(imp)
    imp_t = _dot_nt(pair_t, imp_hi) + _dot_nt(pair_t, imp_lo)
    pos_t = qi * tq + lax.broadcasted_iota(jnp.int32, (1, tq), 1)
    blk = lax.broadcasted_iota(jnp.int32, (n_sel, 1), 0)
    cur = pos_t // SEL_BLOCK
    forced = (blk == 0) | (blk == cur) | (blk == cur - 1)
    score = jnp.where(blk * SEL_BLOCK <= pos_t, imp_t + jnp.where(forced, FORCE_BONUS, 0.0), NEG_INF)
    sel_t = _rank_select(score, n_sel).astype(BF16)

    not_sel = (1.0 - sel_t.astype(F32)).astype(BF16)
    blk_of_key = lax.broadcasted_iota(jnp.int32, (n_sel, tq), 1) // SEL_BLOCK
    blk_row = lax.broadcasted_iota(jnp.int32, (n_sel, tq), 0)
    key_off = lax.broadcasted_iota(jnp.int32, (1, tq), 1)
    q4 = jnp.concatenate(qm, axis=0)

    def sel_body(kt, carry):
        m, l, acc = carry
        start = pl.multiple_of(kt * tq, tq)
        tags = jnp.where(blk_of_key + kt * SEL_PER_TILE == blk_row, NEG_INF, 0.0).astype(BF16)
        bias = _dot_tn(not_sel, tags) + jnp.where(kt * tq + key_off <= pos, 0.0, NEG_INF)
        return _flash_tile(q4, ks_ref[pl.ds(start, tq), :], vs_ref[pl.ds(start, tq), :], bias, m, l, acc)

    init = (jnp.full((GROUP, tq, 1), NEG_INF, F32), jnp.zeros((GROUP, tq, 1), F32),
            [jnp.zeros((tq, LANES), F32)] * (GROUP // 2))
    m, l, acc = lax.fori_loop(0, qi + 1, sel_body, init)
    o_sel = [acc[j] / jnp.where(lower, l[2 * j], l[2 * j + 1]) for j in range(GROUP // 2)]

    kpos = jnp.concatenate([(qi - d) * tq + key_off for d in range(WIN_TILES)], axis=1)
    diff = pos - kpos
    bias = jnp.where((diff >= 0) & (diff < WINDOW) & (kpos >= 0), 0.0, NEG_INF)
    kw = jnp.concatenate([kw0_ref[...], kw1_ref[...], kw2_ref[...]], axis=0)
    vw = jnp.concatenate([vw0_ref[...], vw1_ref[...], vw2_ref[...]], axis=0)
    m, l, acc = _flash_tile(q4, kw, vw, bias, *init)
    o_win = [acc[j] / jnp.where(lower, l[2 * j], l[2 * j + 1]) for j in range(GROUP // 2)]

    for j in range(GROUP // 2):
        sl = slice(j * LANES, (j + 1) * LANES)
        o = bg0_ref[:, sl] * o_cmp[j] + bg1_ref[:, sl] * o_sel[j] + bg2_ref[:, sl] * o_win[j]
        o_ref[:, sl] = o.astype(BF16)


def _nsa_prompt(q, kc, vc, kdup_s, vdup_s, kdup_w, vdup_w, bgx, n, t):
    tq = NSA_TQ
    nq = t // tq
    nc = t // CMP_BLOCK
    hw = GROUP * HEAD_DIM
    qspec = pl.BlockSpec((tq, hw), lambda i, j, h: (i * nq + j, h))
    cspec = pl.BlockSpec((nc, LANES), lambda i, j, h: (i, h // 2))
    seq_spec = pl.BlockSpec((t, LANES), lambda i, j, h: (i, h))
    win_specs = [pl.BlockSpec((tq, LANES), functools.partial(lambda d, i, j, h: (i * nq + jnp.maximum(j - d, 0), h), d))
                 for d in range(WIN_TILES)]
    bg_specs = [pl.BlockSpec((tq, hw), functools.partial(lambda b, i, j, h: (i * nq + j, b * N_KV_HEADS + h), b))
                for b in range(3)]
    return pl.pallas_call(
        _nsa_prompt_kernel,
        grid=(n, nq, N_KV_HEADS),
        in_specs=[qspec, cspec, cspec, seq_spec, seq_spec] + win_specs + win_specs + bg_specs,
        out_specs=qspec,
        out_shape=jax.ShapeDtypeStruct((n * t, C_Q), BF16),
        compiler_params=_params(("parallel", "parallel", "arbitrary")),
        name="nsa_prompt",
    )(q, kc, vc, kdup_s, vdup_s, kdup_w, kdup_w, kdup_w, vdup_w, vdup_w, vdup_w, bgx, bgx, bgx)


HD = N_KV_HEADS * HEAD_DIM
CP_PAGES = 32
BLOCKS_PER_PAGE = PAGE_SIZE // CMP_BLOCK
DEC_T = 4
STEP_ROWS = N_KV_HEADS * DEC_T * GROUP
SUB_PAGES = 32
PAGES_PER_SEQ = 64
NEW_ROWS = 8
LOWEST = -3.0e38


def _native_pool(cache_layer):
    n_pool = cache_layer.shape[0]
    return jnp.transpose(cache_layer, (0, 2, 3, 4, 1)).reshape(n_pool, 2, HD, PAGE_SIZE)


def _slab_copy(pool_ref, pt_ref, buf_ref, sem, seq, first_page, slot, i, s):
    page = pt_ref[seq, first_page + i]
    dst = buf_ref.at[slot, pl.ds((s * CP_PAGES + i) * HD, HD), :]
    return pltpu.make_async_copy(pool_ref.at[page, s], dst, sem.at[slot])


def _fetch_pages(pool_ref, pt_ref, buf_ref, sem, seq, first_page, slot, wait):
    for i in range(CP_PAGES):
        for s in range(2):
            cp = _slab_copy(pool_ref, pt_ref, buf_ref, sem, seq, first_page, slot, i, s)
            if wait:
                cp.wait()
            else:
                cp.start()


def _compress_pages_kernel(pt_ref, pool_ref, w1_hbm, pe_ref, w2_ref, kg_ref, kc_ref, vc_ref, buf_ref, w1_ref, sem, wsem):
    step = pl.program_id(0)
    n_steps = pl.num_programs(0)
    halves = pt_ref.shape[1] // CP_PAGES
    fetch = functools.partial(_fetch_pages, pool_ref, pt_ref, buf_ref, sem)

    @pl.when(step == 0)
    def _():
        cp = pltpu.make_async_copy(w1_hbm, w1_ref, wsem)
        cp.start()
        fetch(0, 0, 0, False)
        cp.wait()

    for slot in range(2):
        @pl.when(step % 2 == slot)
        def _():
            @pl.when(step + 1 < n_steps)
            def _():
                nxt = step + 1
                fetch(nxt // halves, (nxt % halves) * CP_PAGES, 1 - slot, False)

            fetch(step // halves, (step % halves) * CP_PAGES, slot, True)
            ones_bd = _head_pair_ones()
            for s, out_ref in ((0, kc_ref), (1, vc_ref)):
                acc = jnp.zeros((N_KV_HEADS * CP_PAGES, BLOCKS_PER_PAGE * CMP_HIDDEN), F32)
                for dp in range(HEAD_DIM // 2):
                    lanes = []
                    for d in (2 * dp, 2 * dp + 1):
                        rows = [buf_ref[slot, pl.ds(s * CP_PAGES * HD + h * HEAD_DIM + d, CP_PAGES, stride=HD), :]
                                for h in range(N_KV_HEADS)]
                        lanes.append(jnp.concatenate(rows, axis=0))
                    acc = acc + _dot(jnp.concatenate(lanes, axis=1).astype(BF16), w1_ref[s, dp])
                hdn = _gelu(acc + pe_ref[s:s + 1, :])
                out = _dot(hdn.astype(BF16), w2_ref[s])
                if s == 0:
                    tiles = []
                    for j in range(out.shape[1] // LANES):
                        part = out[:, j * LANES:(j + 1) * LANES]
                        hi, lo = _split_bf16(part * part)
                        ss = _dot(hi, ones_bd) + _dot(lo, ones_bd)
                        tiles.append(part * lax.rsqrt(ss * (1.0 / HEAD_DIM) + NORM_EPS) * kg_ref[...])
                    out = jnp.concatenate(tiles, axis=1)
                for h in range(N_KV_HEADS):
                    out_ref[0, h] = out[h * CP_PAGES:(h + 1) * CP_PAGES, :]


def _compress_pages(pool_t, page_table, w):
    n_seq, n_pages = page_table.shape
    halves = n_pages // CP_PAGES
    consts = (w['cmp_pe4'], w['cmp_w2bd4'], w['k_gain2'][0:1])
    out = pl.BlockSpec((1, N_KV_HEADS, CP_PAGES, BLOCKS_PER_PAGE * HEAD_DIM), lambda i, pt: (i // halves, 0, i % halves, 0))
    grid_spec = pltpu.PrefetchScalarGridSpec(
        num_scalar_prefetch=1,
        grid=(n_seq * halves,),
        in_specs=[pl.BlockSpec(memory_space=pl.ANY), pl.BlockSpec(memory_space=pl.ANY)]
        + [pl.BlockSpec(c.shape, functools.partial(lambda nd, i, pt: (0,) * nd, len(c.shape))) for c in consts],
        out_specs=[out, out],
        scratch_shapes=[pltpu.VMEM((2, 2 * CP_PAGES * HD, PAGE_SIZE), F32), pltpu.VMEM(w['cmp_w1pair'].shape, BF16),
                        pltpu.SemaphoreType.DMA((2,)), pltpu.SemaphoreType.DMA(())],
    )
    shape = jax.ShapeDtypeStruct((n_seq, N_KV_HEADS, n_pages, BLOCKS_PER_PAGE * HEAD_DIM), F32)
    return pl.pallas_call(
        _compress_pages_kernel,
        grid_spec=grid_spec,
        out_shape=[shape, shape],
        compiler_params=_params(("arbitrary",)),
        name="compress_pages",
    )(page_table, pool_t, w['cmp_w1pair'], *consts)


def _diag_heads(o):
    rows = STEP_ROWS // N_KV_HEADS
    parts = []
    for h in range(N_KV_HEADS):
        tile = o[h * rows:(h + 1) * rows, (h // 2) * LANES:(h // 2 + 1) * LANES]
        parts.append(pltpu.roll(tile, HEAD_DIM, 1) if h % 2 else tile)
    return jnp.concatenate(parts, axis=0)


def _topk_columns(score, score_new, idx, idx_new, k):
    sel = jnp.zeros(score.shape, F32)
    sel_new = jnp.zeros(score_new.shape, F32)
    big = jnp.float32(1.0e9)
    for _ in range(k):
        mx = jnp.maximum(jnp.max(score, axis=0, keepdims=True), score_new)
        first = jnp.minimum(jnp.min(jnp.where(score == mx, idx, big), axis=0, keepdims=True),
                            jnp.where(score_new == mx, idx_new, big))
        pick = idx == first
        pick_new = first == idx_new
        sel = jnp.where(pick, 1.0, sel)
        sel_new = jnp.where(pick_new, 1.0, sel_new)
        score = jnp.where(pick, LOWEST, score)
        score_new = jnp.where(pick_new, LOWEST, score_new)
    return sel, sel_new


def _nsa_step_kernel(pt_ref, q_ref, gate_ref, kc_ref, vc_ref, snew_ref, wnew_ref, wnewt_ref, win_ref, exp_ref, pool_ref,
                     x_ref, wout_ref, buf_ref, sem):
    seq = pl.program_id(0)
    n_seq = pl.num_programs(0)
    n_pages = pt_ref.shape[1]
    past_len = n_pages * PAGE_SIZE
    n_past_blk = past_len // SEL_BLOCK
    fetch = functools.partial(_fetch_pages, pool_ref, pt_ref, buf_ref, sem)

    @pl.when(seq == 0)
    def _():
        fetch(0, 0, 0, False)

    fetch(seq, CP_PAGES, 1, False)

    q = q_ref[0]
    row = lax.broadcasted_iota(jnp.int32, (STEP_ROWS, 1), 0)
    t_row = (row // GROUP) % DEC_T
    qpos = past_len + t_row
    lane = lax.broadcasted_iota(jnp.int32, (1, LANES), 1)
    lower = lane < HEAD_DIM
    rows_h = STEP_ROWS // N_KV_HEADS

    c_even = BLOCKS_PER_PAGE * (lane % PAGES_PER_SEQ) + 2 * (lane // PAGES_PER_SEQ)
    zero_q = jnp.zeros((STEP_ROWS, LANES), BF16)
    s_even = jnp.zeros((STEP_ROWS, LANES), F32)
    s_odd = jnp.zeros((STEP_ROWS, LANES), F32)
    kv_cat = []
    for h in range(N_KV_HEADS):
        tile = q[:, (h // 2) * LANES:(h // 2 + 1) * LANES]
        swapped = pltpu.roll(tile.astype(F32), HEAD_DIM, 1).astype(BF16)
        q_lo = jnp.where(lower, swapped if h % 2 else tile, zero_q)
        q_hi = jnp.where(lower, zero_q, tile if h % 2 else swapped)
        kcat = jnp.concatenate([kc_ref[0, h, :, 0:LANES], kc_ref[0, h, :, LANES:2 * LANES]], axis=0).astype(BF16)
        vcat = jnp.concatenate([vc_ref[0, h, :, 0:LANES], vc_ref[0, h, :, LANES:2 * LANES]], axis=0)
        kv_cat.append(vcat)
        s_even = s_even + _dot_nt(q_lo, kcat)
        s_odd = s_odd + _dot_nt(q_hi, kcat)
    m_even = (c_even + 1) * CMP_BLOCK - 1 <= qpos
    m_odd = (c_even + 2) * CMP_BLOCK - 1 <= qpos
    s_even = jnp.where(m_even, s_even, NEG_INF)
    s_odd = jnp.where(m_odd, s_odd, NEG_INF)
    mx = jnp.maximum(jnp.max(s_even, axis=-1, keepdims=True), jnp.max(s_odd, axis=-1, keepdims=True))
    e_even = jnp.where(m_even, jnp.exp(s_even - mx), 0.0)
    e_odd = jnp.where(m_odd, jnp.exp(s_odd - mx), 0.0)
    den = jnp.maximum(jnp.sum(e_even, axis=-1, keepdims=True) + jnp.sum(e_odd, axis=-1, keepdims=True), TINY)
    pr_even = e_even / den
    pr_odd = e_odd / den
    oc = jnp.zeros((STEP_ROWS, LANES), F32)
    for h in range(N_KV_HEADS):
        mine = (row // rows_h) == h
        oc = (oc + _dot(jnp.where(mine, pr_even, 0.0).astype(BF16), jnp.where(lower, kv_cat[h], 0.0).astype(BF16))
              + _dot(jnp.where(mine, pr_odd, 0.0).astype(BF16), jnp.where(lower, 0.0, kv_cat[h]).astype(BF16)))
    o_cmp = oc + pltpu.roll(oc, HEAD_DIM, 1)
    pr_sum = [pr_even + pr_odd]

    group_sum = jnp.where(lax.broadcasted_iota(jnp.int32, (STEP_ROWS, STEP_ROWS), 0) // GROUP
                          == lax.broadcasted_iota(jnp.int32, (STEP_ROWS, STEP_ROWS), 1) // GROUP, 1.0, 0.0).astype(BF16)
    p_hi, p_lo = _split_bf16(jnp.concatenate(pr_sum, axis=0))
    imp_t = _dot_tn(p_hi, group_sum) + _dot_tn(p_lo, group_sum)
    sub = lax.broadcasted_iota(jnp.int32, (LANES, 1), 0)
    blk = 2 * (sub % PAGES_PER_SEQ) + sub // PAGES_PER_SEQ
    col = lax.broadcasted_iota(jnp.int32, (1, STEP_ROWS), 1)
    qpos_c = past_len + (col // GROUP) % DEC_T
    cur = qpos_c // SEL_BLOCK
    forced = (blk == 0) | (blk == cur) | (blk == cur - 1)
    score = jnp.where(blk * SEL_BLOCK <= qpos_c, imp_t + jnp.where(forced, FORCE_BONUS, 0.0), NEG_INF)
    forced_new = (n_past_blk == cur) | (n_past_blk == cur - 1)
    score_new = jnp.where(n_past_blk * SEL_BLOCK <= qpos_c, jnp.where(forced_new, FORCE_BONUS, 0.0), NEG_INF)
    sel_t, sel_new_t = _topk_columns(score, score_new, blk.astype(F32), jnp.float32(n_past_blk), SEL_TOPK)
    bias_t = jnp.where(sel_t > 0.5, 0.0, NEG_INF).astype(BF16)
    flag_rows = jnp.where(lax.broadcasted_iota(jnp.int32, (NEW_ROWS, 1), 0) == 0, sel_new_t, 0.0).astype(BF16)
    sel_new = _dot_tn(flag_rows, jnp.ones((NEW_ROWS, LANES), BF16))[:, 0:1]

    def page_tiles(slot, page0, carry):
        m, l, acc = carry
        for sc in range(CP_PAGES // SUB_PAGES):
            pages = [sc * SUB_PAGES + i for i in range(0, SUB_PAGES, 2)]
            slab2 = lambda s, i: jnp.concatenate([buf_ref[slot, pl.ds((s * CP_PAGES + i) * HD, HD), :].astype(BF16),
                                                  buf_ref[slot, pl.ds((s * CP_PAGES + i + 1) * HD, HD), :].astype(BF16)], axis=1)
            s_t = jnp.concatenate([_dot(q, slab2(0, i)) for i in pages], axis=1)
            first_key = (page0 + sc * SUB_PAGES) * PAGE_SIZE
            s_t = s_t + _dot_tn(bias_t, exp_ref[:, first_key:first_key + SUB_PAGES * PAGE_SIZE])
            m_new = jnp.maximum(m, jnp.max(s_t, axis=-1, keepdims=True))
            alpha = jnp.exp(m - m_new)
            p = jnp.exp(s_t - m_new)
            l = alpha * l + jnp.sum(p, axis=-1, keepdims=True)
            acc = acc * alpha
            for k, i in enumerate(pages):
                acc = acc + _dot_nt(p[:, 2 * k * PAGE_SIZE:2 * (k + 1) * PAGE_SIZE].astype(BF16), slab2(1, i))
            m = m_new
        return m, l, acc

    carry = (jnp.full((STEP_ROWS, 1), NEG_INF, F32), jnp.zeros((STEP_ROWS, 1), F32), jnp.zeros((STEP_ROWS, HD), F32))
    fetch(seq, 0, 0, True)
    carry = page_tiles(0, 0, carry)

    @pl.when(seq + 1 < n_seq)
    def _():
        fetch(seq + 1, 0, 0, False)

    fetch(seq, CP_PAGES, 1, True)
    m, l, acc = page_tiles(1, CP_PAGES, carry)

    new_vis = lax.broadcasted_iota(jnp.int32, (1, NEW_ROWS), 1) <= t_row
    snew = snew_ref[0]
    s_n = jnp.where(new_vis & (sel_new > 0.5), _dot_nt(q, snew[:, 0:HD].astype(BF16)), NEG_INF)
    m_new = jnp.maximum(m, jnp.max(s_n, axis=-1, keepdims=True))
    alpha = jnp.exp(m - m_new)
    p_n = jnp.exp(s_n - m_new)
    l = alpha * l + jnp.sum(p_n, axis=-1, keepdims=True)
    acc = acc * alpha + _dot(p_n.astype(BF16), snew[:, HD:2 * HD].astype(BF16))
    o_sel = _diag_heads(acc / l)

    lb = win_ref.shape[3]
    kpos = past_len - lb + lax.broadcasted_iota(jnp.int32, (1, lb), 1)
    diff = qpos - kpos
    s_w = jnp.where((diff >= 0) & (diff < WINDOW) & (kpos >= 0), _dot(q, win_ref[0, 0].astype(BF16)), NEG_INF)
    wnew = wnew_ref[0]
    s_wn = jnp.where(new_vis, _dot_nt(q, wnew[:, 0:HD].astype(BF16)), NEG_INF)
    mw = jnp.maximum(jnp.max(s_w, axis=-1, keepdims=True), jnp.max(s_wn, axis=-1, keepdims=True))
    p_w = jnp.exp(s_w - mw)
    p_wn = jnp.exp(s_wn - mw)
    l_w = jnp.sum(p_w, axis=-1, keepdims=True) + jnp.sum(p_wn, axis=-1, keepdims=True)
    acc_w = _dot_nt(p_w.astype(BF16), win_ref[0, 1].astype(BF16)) + _dot(p_wn.astype(BF16), wnew[:, HD:2 * HD].astype(BF16))
    o_win = _diag_heads(acc_w / l_w)

    gate = gate_ref[0]
    x_ref[0] = gate[:, 0:1] * o_cmp + gate[:, 1:2] * o_sel + gate[:, 2:3] * o_win

    shift = LANES - DEC_T
    keep = lane < shift
    n_tiles = lb // LANES
    for s in range(2):
        for k in range(n_tiles):
            cur_tile = pltpu.roll(win_ref[0, s, :, k * LANES:(k + 1) * LANES], shift, 1)
            nxt_tile = (pltpu.roll(win_ref[0, s, :, (k + 1) * LANES:(k + 2) * LANES], shift, 1) if k + 1 < n_tiles
                        else wnewt_ref[0, s])
            wout_ref[0, s, :, k * LANES:(k + 1) * LANES] = jnp.where(keep, cur_tile, nxt_tile)


def _nsa_step(q_bd, gates, kc, vc, snew8, wnew8, wnew_t, win_t, expand, pool_t, page_table):
    n_seq, n_pages = page_table.shape
    lb = win_t.shape[3]
    per_seq = lambda shape: pl.BlockSpec((1,) + shape, functools.partial(lambda nd, i, pt: (i,) + (0,) * nd, len(shape)))
    cshape = (N_KV_HEADS, n_pages, BLOCKS_PER_PAGE * HEAD_DIM)
    grid_spec = pltpu.PrefetchScalarGridSpec(
        num_scalar_prefetch=1,
        grid=(n_seq,),
        in_specs=[per_seq((STEP_ROWS, HD)), per_seq((STEP_ROWS, LANES)), per_seq(cshape), per_seq(cshape),
                  per_seq((NEW_ROWS, KV_ROW)), per_seq((NEW_ROWS, KV_ROW)), per_seq((2, HD, LANES)), per_seq((2, HD, lb)),
                  pl.BlockSpec(expand.shape, lambda i, pt: (0, 0)), pl.BlockSpec(memory_space=pl.ANY)],
        out_specs=[per_seq((STEP_ROWS, LANES)), per_seq((2, HD, lb))],
        scratch_shapes=[pltpu.VMEM((2, 2 * CP_PAGES * HD, PAGE_SIZE), F32), pltpu.SemaphoreType.DMA((2,))],
    )
    return pl.pallas_call(
        _nsa_step_kernel,
        grid_spec=grid_spec,
        out_shape=[jax.ShapeDtypeStruct((n_seq, STEP_ROWS, LANES), F32), jax.ShapeDtypeStruct((n_seq, 2, HD, lb), F32)],
        compiler_params=_params(("arbitrary",)),
        name="nsa_step",
    )(page_table, q_bd, gates, kc, vc, snew8, wnew8, wnew_t, win_t, expand, pool_t)


def _step_query_layout(q_s, n_seq):
    q5 = q_s.reshape(n_seq, DEC_T, N_KV_HEADS, GROUP, HEAD_DIM)
    qt = jnp.transpose(q5, (0, 2, 1, 3, 4)).reshape(n_seq, N_KV_HEADS, DEC_T * GROUP, HEAD_DIM)
    eye = jnp.eye(N_KV_HEADS, dtype=q_s.dtype)
    return jnp.einsum('nhrd,hk->nhrkd', qt, eye).reshape(n_seq, STEP_ROWS, HD)


def _step_gate_layout(bgx_s, n_seq):
    bg = bgx_s.reshape(n_seq, DEC_T, 3, N_KV_HEADS, GROUP, HEAD_DIM)[..., 0]
    bg = jnp.transpose(bg, (0, 3, 1, 4, 2)).reshape(n_seq, STEP_ROWS, 3)
    return jnp.pad(bg, ((0, 0), (0, 0), (0, LANES - 3)))


def _step_output_layout(x, n_seq):
    x5 = x[:, :, :HEAD_DIM].reshape(n_seq, N_KV_HEADS, DEC_T, GROUP, HEAD_DIM)
    return jnp.transpose(x5, (0, 2, 1, 3, 4)).reshape(n_seq * DEC_T, C_Q)


def _block_expand_matrix(n_pages):
    rows = jnp.arange(LANES)
    keys = jnp.arange(n_pages * PAGE_SIZE)
    hit = (((rows[:, None] % PAGES_PER_SEQ) == (keys[None, :] // PAGE_SIZE))
           & ((rows[:, None] // PAGES_PER_SEQ) == ((keys[None, :] % PAGE_SIZE) // SEL_BLOCK)))
    return hit.astype(BF16)


PEER_HALF = PEER_KEY_DIM // 2
N_SIDES = 2 * PEER_HEADS


def _merge_kernel(x_ref, conv_ref, o_ref, gmix_ref, wmg_ref, wao_ref, wo_ref, gffn_ref, wq_ref, sk_ref,
                  h_ref, hn_ref, st_ref):
    x = x_ref[...]
    n = (x * lax.rsqrt(jnp.mean(x * x, axis=-1, keepdims=True) + NORM_EPS) * gmix_ref[...]).astype(BF16)
    mg = _sigmoid(_dot(n, wmg_ref[...]))
    attn = _dot(o_ref[...], wao_ref[...])
    mix = mg[:, :D_MODEL] * conv_ref[...] + mg[:, D_MODEL:] * attn
    h = x + _dot(mix.astype(BF16), wo_ref[...])
    h_ref[...] = h
    hn = (h * lax.rsqrt(jnp.mean(h * h, axis=-1, keepdims=True) + NORM_EPS) * gffn_ref[...]).astype(BF16)
    hn_ref[...] = hn
    qp = _dot(hn, wq_ref[...]).astype(BF16)
    for i in range(N_SIDES):
        st_ref[i * N_KEYS:(i + 1) * N_KEYS, :] = _dot_nt(sk_ref[i], qp[:, i * PEER_HALF:(i + 1) * PEER_HALF])


def _merge(x2d, conv_out, o, w, row_tile):
    ntok = x2d.shape[0]
    row = lambda width: pl.BlockSpec((row_tile, width), lambda i: (i, 0))
    consts = (w['g_mix'], w['w_mg'], w['w_attn_out'], w['w_o'], w['g_ffn'], w['peer_wq'], w['peer_subkeys'])
    return pl.pallas_call(
        _merge_kernel,
        grid=(ntok // row_tile,),
        in_specs=[row(D_MODEL), row(D_MODEL), row(C_Q)] + [_const_spec(c.shape) for c in consts],
        out_specs=[row(D_MODEL), row(D_MODEL), pl.BlockSpec((N_SIDES * N_KEYS, row_tile), lambda i: (0, i))],
        out_shape=[jax.ShapeDtypeStruct((ntok, D_MODEL), F32), jax.ShapeDtypeStruct((ntok, D_MODEL), BF16),
                   jax.ShapeDtypeStruct((N_SIDES * N_KEYS, ntok), F32)],
        compiler_params=_params(("parallel",)),
        name="merge",
    )(x2d, conv_out, o, *consts)


def _cand_groups():
    groups = []
    for j in range(PEER_TOPK):
        n_valid = PEER_TOPK // (j + 1)
        groups.append((j, n_valid, -(-n_valid // 8) * 8))
    return groups


def _top_rows_sorted(s, k):
    n = s.shape[0]
    row = lax.broadcasted_iota(jnp.int32, (n, 1), 0).astype(F32)
    out_row = lax.broadcasted_iota(jnp.int32, (k, 1), 0)
    vals = jnp.zeros((k, s.shape[1]), F32)
    idxs = jnp.zeros((k, s.shape[1]), F32)
    for r in range(k):
        mx = jnp.max(s, axis=0, keepdims=True)
        first = jnp.min(jnp.where(s == mx, row, float(n)), axis=0, keepdims=True)
        vals = jnp.where(out_row == r, mx, vals)
        idxs = jnp.where(out_row == r, first, idxs)
        s = jnp.where(row == first, LOWEST, s)
    return vals, idxs


def _peer_topk_kernel(st_ref, a_ref, b_ref, g_ref):
    tt = st_ref.shape[1]
    groups = _cand_groups()
    row16 = lax.broadcasted_iota(jnp.int32, (PEER_TOPK, 1), 0).astype(F32)
    a_rows, b_rows, g_rows = [], [], []
    for hd in range(PEER_HEADS):
        va, ia = _top_rows_sorted(st_ref[(2 * hd) * N_KEYS:(2 * hd + 1) * N_KEYS, :], PEER_TOPK)
        vb, ib = _top_rows_sorted(st_ref[(2 * hd + 1) * N_KEYS:(2 * hd + 2) * N_KEYS, :], PEER_TOPK)
        cands, flats = [], []
        for j, n_valid, n_rows in groups:
            i_idx = lax.broadcasted_iota(jnp.int32, (n_rows, 1), 0)
            cands.append(jnp.where(i_idx < n_valid, va[0:n_rows, :] + vb[j:j + 1, :], LOWEST))
            flats.append((i_idx * PEER_TOPK + j).astype(F32))
        cand = jnp.concatenate(cands, axis=0)
        flat = jnp.concatenate(flats, axis=0)
        n_flat = float(PEER_TOPK * PEER_TOPK)
        sc = jnp.zeros((PEER_TOPK, tt), F32)
        ea = jnp.zeros((PEER_TOPK, tt), F32)
        eb = jnp.zeros((PEER_TOPK, tt), F32)
        for r in range(PEER_TOPK):
            mx = jnp.max(cand, axis=0, keepdims=True)
            first = jnp.min(jnp.where(cand == mx, flat, n_flat), axis=0, keepdims=True)
            cand = jnp.where(flat == first, LOWEST, cand)
            i_sel = jnp.floor(first * (1.0 / PEER_TOPK))
            j_sel = first - i_sel * PEER_TOPK
            a_sel = jnp.sum(jnp.where(row16 == i_sel, ia, 0.0), axis=0, keepdims=True)
            b_sel = jnp.sum(jnp.where(row16 == j_sel, ib, 0.0), axis=0, keepdims=True)
            sc = jnp.where(row16 == r, mx, sc)
            ea = jnp.where(row16 == r, a_sel, ea)
            eb = jnp.where(row16 == r, b_sel, eb)
        e = jnp.exp(sc - sc[0:1, :])
        g_rows.append(e / jnp.sum(e, axis=0, keepdims=True))
        a_rows.append(ea)
        b_rows.append(eb)
    a_ref[...] = jnp.transpose(jnp.concatenate(a_rows, axis=0))
    b_ref[...] = jnp.transpose(jnp.concatenate(b_rows, axis=0))
    g_ref[...] = jnp.transpose(jnp.concatenate(g_rows, axis=0))


def _peer_topk(scores_t, tok_tile):
    ntok = scores_t.shape[1]
    n_pick = PEER_HEADS * PEER_TOPK
    out = pl.BlockSpec((tok_tile, n_pick), lambda i: (i, 0))
    return pl.pallas_call(
        _peer_topk_kernel,
        grid=(ntok // tok_tile,),
        in_specs=[pl.BlockSpec((N_SIDES * N_KEYS, tok_tile), lambda i: (0, i))],
        out_specs=[out, out, out],
        out_shape=[jax.ShapeDtypeStruct((ntok, n_pick), F32)] * 3,
        compiler_params=_params(("parallel",)),
        name="peer_topk",
    )(scores_t)


W_TOK = 64


def _peer_w_kernel(a_ref, b_ref, g_ref, w_ref):
    sub = lax.broadcasted_iota(jnp.int32, (N_KEYS, 1), 0).astype(F32)

    def body(t, carry):
        a_row = a_ref[pl.ds(t, 1), :]
        b_row = b_ref[pl.ds(t, 1), :]
        g_row = g_ref[pl.ds(t, 1), :]
        ga = jnp.where(sub == a_row, g_row, 0.0).astype(BF16)
        ob = jnp.where(sub == b_row, 1.0, 0.0).astype(BF16)
        w_ref[t] = _dot_nt(ga, ob)
        return carry

    lax.fori_loop(0, W_TOK, body, 0, unroll=4)


def _peer_weights(a_idx, b_idx, gate):
    ntok, n_pick = a_idx.shape
    row = pl.BlockSpec((W_TOK, n_pick), lambda i: (i, 0))
    return pl.pallas_call(
        _peer_w_kernel,
        grid=(ntok // W_TOK,),
        in_specs=[row, row, row],
        out_specs=pl.BlockSpec((W_TOK, N_KEYS, N_KEYS), lambda i: (i, 0, 0)),
        out_shape=jax.ShapeDtypeStruct((ntok, N_KEYS, N_KEYS), F32),
        compiler_params=_params(("parallel",)),
        name="peer_weights",
    )(a_idx, b_idx, gate)


PEER_TOK = 512
PEER_ABLK = 16


def _peer_dense_kernel(hn_ref, h_ref, w_ref, u_ref, v_ref, y_ref, acc_ref):
    j = pl.program_id(1)

    @pl.when(j == 0)
    def _():
        acc_ref[...] = jnp.zeros(acc_ref.shape, F32)

    hn = hn_ref[...]
    pair = 2 * N_KEYS
    for k in range(PEER_ABLK // 2):
        act = _dot_nt(hn, u_ref[k * pair:(k + 1) * pair, :])
        wk = jnp.concatenate([w_ref[:, 2 * k, :], w_ref[:, 2 * k + 1, :]], axis=1)
        acc_ref[...] += _dot((wk * _gelu(act)).astype(BF16), v_ref[k * pair:(k + 1) * pair, :])

    @pl.when(j == pl.num_programs(1) - 1)
    def _():
        y_ref[...] = h_ref[...] + acc_ref[...]


def _peer_dense(hn, h, w_atb, u, v):
    ntok = hn.shape[0]
    blk = PEER_ABLK * N_KEYS
    tok = lambda: pl.BlockSpec((PEER_TOK, D_MODEL), lambda i, j: (i, 0))
    exp = lambda: pl.BlockSpec((blk, D_MODEL), lambda i, j: (j, 0))
    return pl.pallas_call(
        _peer_dense_kernel,
        grid=(ntok // PEER_TOK, N_KEYS // PEER_ABLK),
        in_specs=[tok(), tok(), pl.BlockSpec((PEER_TOK, PEER_ABLK, N_KEYS), lambda i, j: (i, j, 0)), exp(), exp()],
        out_specs=tok(),
        out_shape=jax.ShapeDtypeStruct((ntok, D_MODEL), F32),
        scratch_shapes=[pltpu.VMEM((PEER_TOK, D_MODEL), F32)],
        compiler_params=_params(("parallel", "arbitrary")),
        name="peer_dense",
    )(hn, h, w_atb, u, v)


def kernel(x_prompt, x_sample, cache_cmp_kv, cache_sel_kv, state_win_kv, state_conv, page_table, g_mix, w_in, conv_w, conv_b, conv_ln_g, conv_ln_b, w_conv_out, q_gain, k_gain, cmp_pe, cmp_w1, cmp_w2, w_attn_out, w_o, g_ffn, peer_wq, peer_subkeys, peer_u, peer_v):
    layer = 0
    p = {'g_mix': g_mix, 'w_in': w_in, 'conv_w': conv_w, 'conv_b': conv_b, 'conv_ln_g': conv_ln_g,
         'conv_ln_b': conv_ln_b, 'w_conv_out': w_conv_out, 'q_gain': q_gain, 'k_gain': k_gain, 'cmp_pe': cmp_pe,
         'cmp_w1': cmp_w1, 'cmp_w2': cmp_w2, 'w_attn_out': w_attn_out, 'w_o': w_o, 'g_ffn': g_ffn,
         'peer_wq': peer_wq, 'peer_subkeys': peer_subkeys}
    w = _prep_weights({k: v[layer] for k, v in p.items()})
    u_bf = peer_u[layer].astype(BF16)
    v_bf = peer_v[layer].astype(BF16)

    def ffn(x2d, conv_out, o):
        h, hn, scores_t = _merge(x2d, conv_out, o, w, ROW_TILE)
        a_idx, b_idx, gate = _peer_topk(scores_t, ROW_TILE)
        return _peer_dense(hn, h, _peer_weights(a_idx, b_idx, gate), u_bf, v_bf)

    n, t, _ = x_prompt.shape
    xp = x_prompt.reshape(n * t, D_MODEL)
    cos_p, sin_p = _rope_tables(jnp.arange(t, dtype=jnp.int32))
    glu, q, ckv, _, _, bgx, kds, vds, kdw, vdw, ckv_t, skv_t, wkv_t = _project(xp, cos_p, sin_p, w, ROW_TILE, t)
    glu3 = glu.reshape(n, t, D_CONV)
    conv_p = _conv_prompt(glu3, w, ROW_TILE).reshape(n * t, D_MODEL)
    kc, vc = _compress_prompt(ckv, w, 128)
    o_p = _nsa_prompt(q, kc, vc, kds, vds, kdw, vdw, bgx, n, t)
    y_p = ffn(xp, conv_p, o_p).reshape(n, t, D_MODEL)

    n_s, t_s, _ = x_sample.shape
    past_len = page_table.shape[1] * PAGE_SIZE
    xs = x_sample.reshape(n_s * t_s, D_MODEL)
    pos_s = past_len + jnp.arange(t_s, dtype=jnp.int32)
    cos_s, sin_s = _rope_tables(jnp.tile(pos_s, ROW_TILE // t_s))
    glu_s, q_s, ckv_s, skv_s, wkv_s, bgx_s = _project(xs, cos_s, sin_s, w, ROW_TILE, ROW_TILE)[:6]
    glu_s3 = glu_s.reshape(n_s, t_s, D_CONV)
    conv_s = _conv_step(jnp.swapaxes(state_conv[layer], 0, 1), jnp.swapaxes(glu_s3, 0, 1), w)
    conv_s = jnp.swapaxes(conv_s, 0, 1).reshape(n_s * t_s, D_MODEL)
    assert t_s == DEC_T and page_table.shape[1] == PAGES_PER_SEQ
    kc_s, vc_s = _compress_pages(_native_pool(cache_cmp_kv[layer]), page_table, w)
    lb = state_win_kv.shape[2]
    win_t = jnp.transpose(state_win_kv[layer], (0, 2, 3, 4, 1)).reshape(n_s, 2, HD, lb)
    pad_new = lambda a: jnp.pad(a.reshape(n_s, t_s, KV_ROW), ((0, 0), (0, NEW_ROWS - t_s), (0, 0)))
    wnew_t = jnp.pad(jnp.transpose(wkv_s.reshape(n_s, t_s, 2, HD), (0, 2, 3, 1)), ((0, 0), (0, 0), (0, 0), (LANES - t_s, 0)))
    x_step, win_s = _nsa_step(_step_query_layout(q_s, n_s), _step_gate_layout(bgx_s, n_s), kc_s, vc_s,
                              pad_new(skv_s), pad_new(wkv_s), wnew_t, win_t, _block_expand_matrix(PAGES_PER_SEQ),
                              _native_pool(cache_sel_kv[layer]), page_table)
    o_s = _step_output_layout(x_step, n_s).astype(BF16)
    y_s = ffn(xs, conv_s, o_s).reshape(n_s, t_s, D_MODEL)
    win_s = jnp.transpose(win_s.reshape(n_s, 2, N_KV_HEADS, HEAD_DIM, lb), (0, 4, 1, 2, 3))

    kv6 = lambda a, nn, tt: a.reshape(1, nn, tt, 2, N_KV_HEADS, HEAD_DIM)
    kv6t = lambda a: jnp.transpose(a.reshape(a.shape[0], 2, N_KV_HEADS, HEAD_DIM, a.shape[2]), (0, 4, 1, 2, 3))[None]
    n_win = min(WINDOW, t)
    new_conv_s = jnp.concatenate([state_conv[layer][:, t_s:], glu_s3], axis=1)
    return (y_p, y_s,
            kv6t(ckv_t), kv6t(skv_t), kv6t(wkv_t[:, :, t - n_win:]), glu3[None, :, t - (CONV_WIDTH - 1):],
            kv6(ckv_s, n_s, t_s), kv6(skv_s, n_s, t_s), win_s[None], new_conv_s[None])
```

```python
import functools
import math

import jax
import jax.numpy as jnp
from jax import lax
from jax.experimental import pallas as pl
from jax.experimental.pallas import tpu as pltpu

F32 = jnp.float32
BF16 = jnp.bfloat16

D_MODEL = 1024
D_CONV = 512
CONV_WIDTH = 31
N_HEADS = 16
N_KV_HEADS = 4
HEAD_DIM = 64
GROUP = N_HEADS // N_KV_HEADS
CMP_BLOCK = 32
CMP_HIDDEN = 128
SEL_BLOCK = 64
SEL_TOPK = 16
WINDOW = 512
ROPE_THETA = 10000.0
FORCE_BONUS = 1.0e3
N_KEYS = 128
PEER_HEADS = 8
PEER_TOPK = 16
PEER_KEY_DIM = 256
NORM_EPS = 1e-6
NEG_INF = -1.0e30
TINY = 1.0e-30
ATTN_SCALE = HEAD_DIM ** -0.5
KV_ROW = 2 * N_KV_HEADS * HEAD_DIM
C_Q = N_HEADS * HEAD_DIM

ROW_TILE = 256
LANES = 128
VMEM_LIMIT = 56 * 1024 * 1024


def _dot(a, b):
    return jnp.dot(a, b, preferred_element_type=F32)


def _dot_nt(a, b):
    return lax.dot_general(a, b, (((1,), (1,)), ((), ())), preferred_element_type=F32)


def _dot_tn(a, b):
    return lax.dot_general(a, b, (((0,), (0,)), ((), ())), preferred_element_type=F32)


def _split_bf16(x):
    hi = x.astype(BF16)
    lo = (x - hi.astype(F32)).astype(BF16)
    return hi, lo


def _sigmoid(x):
    return 1.0 / (1.0 + jnp.exp(-x))


def _params(sem):
    return pltpu.CompilerParams(dimension_semantics=sem, vmem_limit_bytes=VMEM_LIMIT)


def _const_spec(shape):
    nd = len(shape)
    return pl.BlockSpec(shape, lambda *_: (0,) * nd)


def _head_pair_ones():
    r = lax.broadcasted_iota(jnp.int32, (LANES, LANES), 0) // HEAD_DIM
    c = lax.broadcasted_iota(jnp.int32, (LANES, LANES), 1) // HEAD_DIM
    return jnp.where(r == c, 1.0, 0.0).astype(BF16)


def _head_rms_rope(z, gain, cos, sin_signed, ones_bd, first_half):
    hi, lo = _split_bf16(z * z)
    ss = _dot(hi, ones_bd) + _dot(lo, ones_bd)
    zn = z * lax.rsqrt(ss * (1.0 / HEAD_DIM) + NORM_EPS) * gain
    partner = jnp.where(first_half, pltpu.roll(zn, LANES - HEAD_DIM // 2, 1), pltpu.roll(zn, HEAD_DIM // 2, 1))
    return zn * cos + partner * sin_signed


def _proj_kernel(x_ref, cos_ref, sin_ref, gmix_ref, wglu_ref, wq_ref, wkv_ref, wbg_ref, qg_ref, kg_ref,
                 glu_ref, q_ref, ckv_ref, skv_ref, wkvo_ref, bg_ref, ksd_ref, vsd_ref, kwd_ref, vwd_ref,
                 ckvt_ref, skvt_ref, wkvt_ref):
    dup_refs = (ksd_ref, vsd_ref, kwd_ref, vwd_ref)
    x = x_ref[...]
    ms = jnp.mean(x * x, axis=-1, keepdims=True)
    n = (x * lax.rsqrt(ms + NORM_EPS) * gmix_ref[...]).astype(BF16)
    cos = cos_ref[...]
    sin_signed = sin_ref[...]
    ones_bd = _head_pair_ones()
    lane = lax.broadcasted_iota(jnp.int32, (1, LANES), 1)
    first_half = (lane % HEAD_DIM) < (HEAD_DIM // 2)

    zg = _dot(n, wglu_ref[...])
    glu_ref[...] = zg[:, :D_CONV] * _sigmoid(zg[:, D_CONV:])

    zq = _dot(n, wq_ref[...])
    qg = qg_ref[...]
    for j in range(C_Q // LANES):
        sl = slice(j * LANES, (j + 1) * LANES)
        qj = _head_rms_rope(zq[:, sl], qg, cos, sin_signed, ones_bd, first_half)
        q_ref[:, sl] = (qj * ATTN_SCALE).astype(BF16)

    zkv = _dot(n, wkv_ref[...])
    half = KV_ROW // 2
    lower = lane < HEAD_DIM
    for b, (out_ref, t_ref) in enumerate(((ckv_ref, ckvt_ref), (skv_ref, skvt_ref), (wkvo_ref, wkvt_ref))):
        kg = kg_ref[b:b + 1, :]
        t_ref[0, half:, :] = jnp.transpose(zkv[:, b * KV_ROW + half:(b + 1) * KV_ROW])
        for j in range(half // LANES):
            sl = slice(b * KV_ROW + j * LANES, b * KV_ROW + (j + 1) * LANES)
            kj = _head_rms_rope(zkv[:, sl], kg, cos, sin_signed, ones_bd, first_half)
            out_ref[:, j * LANES:(j + 1) * LANES] = kj
            t_ref[0, j * LANES:(j + 1) * LANES, :] = jnp.transpose(kj)
            vj = zkv[:, b * KV_ROW + half + j * LANES:b * KV_ROW + half + (j + 1) * LANES]
            if b > 0:
                for src, dup_ref in ((kj, dup_refs[2 * (b - 1)]), (vj, dup_refs[2 * (b - 1) + 1])):
                    swapped = pltpu.roll(src, HEAD_DIM, 1)
                    dup_ref[:, 2 * j * LANES:(2 * j + 1) * LANES] = jnp.where(lower, src, swapped).astype(BF16)
                    dup_ref[:, (2 * j + 1) * LANES:(2 * j + 2) * LANES] = jnp.where(lower, swapped, src).astype(BF16)
        out_ref[:, half:] = zkv[:, b * KV_ROW + half:(b + 1) * KV_ROW]

    bg_ref[...] = _sigmoid(_dot(n, wbg_ref[...]))


def _rope_tables(pos):
    half = HEAD_DIM // 2
    inv = jnp.exp(-(2.0 * math.log(ROPE_THETA) / HEAD_DIM) * jnp.arange(half, dtype=F32))
    ang = pos.astype(F32)[:, None] * inv[None, :]
    lane = jnp.arange(LANES)
    cos = jnp.cos(ang)[:, lane % half]
    sin = jnp.sin(ang)[:, lane % half]
    sin_signed = jnp.where((lane % HEAD_DIM) < half, -sin, sin)
    return cos, sin_signed


def _project(x2d, cos, sin_signed, w, row_tile, t_seq):
    ntok = x2d.shape[0]
    n_pos_blocks = cos.shape[0] // row_tile
    nb = t_seq // row_tile
    row = lambda width: pl.BlockSpec((row_tile, width), lambda i: (i, 0))
    tab = pl.BlockSpec((row_tile, LANES), lambda i: (i % n_pos_blocks, 0))
    tr = pl.BlockSpec((1, KV_ROW, row_tile), lambda i: (i // nb, 0, i % nb))
    consts = (w['g_mix'], w['w_glu'], w['w_q'], w['w_kv'], w['w_bgx'], w['q_gain2'], w['k_gain2'])
    return pl.pallas_call(
        _proj_kernel,
        grid=(ntok // row_tile,),
        in_specs=[row(D_MODEL), tab, tab] + [_const_spec(c.shape) for c in consts],
        out_specs=[row(D_CONV), row(C_Q), row(KV_ROW), row(KV_ROW), row(KV_ROW), row(3 * C_Q)] + [row(KV_ROW)] * 4 + [tr] * 3,
        out_shape=[jax.ShapeDtypeStruct((ntok, D_CONV), F32), jax.ShapeDtypeStruct((ntok, C_Q), BF16),
                   jax.ShapeDtypeStruct((ntok, KV_ROW), F32), jax.ShapeDtypeStruct((ntok, KV_ROW), F32),
                   jax.ShapeDtypeStruct((ntok, KV_ROW), F32), jax.ShapeDtypeStruct((ntok, 3 * C_Q), F32)]
        + [jax.ShapeDtypeStruct((ntok, KV_ROW), BF16)] * 4
        + [jax.ShapeDtypeStruct((ntok // t_seq, KV_ROW, t_seq), F32)] * 3,
        compiler_params=_params(("parallel",)),
        name="proj",
    )(x2d, cos, sin_signed, *consts)


def _prep_weights(p):
    w_in = p['w_in']
    c_glu = 2 * D_CONV
    c_kv = 3 * KV_ROW
    c_bg = 3 * N_HEADS
    o1, o2, o3 = c_glu, c_glu + C_Q, c_glu + C_Q + c_kv
    o4 = o3 + c_bg
    row = lambda v: v.reshape(1, -1)
    return {
        'g_mix': row(p['g_mix']),
        'w_glu': w_in[:, :o1].astype(BF16),
        'w_q': w_in[:, o1:o2].astype(BF16),
        'w_kv': w_in[:, o2:o3].astype(BF16),
        'w_bgx': jnp.repeat(w_in[:, o3:o4], HEAD_DIM, axis=1).astype(BF16),
        'w_mg': w_in[:, o4:].astype(BF16),
        'q_gain2': row(jnp.tile(p['q_gain'], 2)),
        'k_gain2': jnp.tile(p['k_gain'], (1, 2)),
        'conv_w': p['conv_w'],
        'conv_b': row(p['conv_b']),
        'conv_ln_g': row(p['conv_ln_g']),
        'conv_ln_b': row(p['conv_ln_b']),
        'w_conv_out': p['w_conv_out'].astype(BF16),
        'cmp_w1bd': _pair_block_diag(p['cmp_w1']).astype(BF16),
        'cmp_w2bd': _pair_block_diag(p['cmp_w2']).astype(BF16),
        'cmp_w1pair': jnp.einsum('scde,bk->sdbcke', p['cmp_w1'], jnp.eye(4, dtype=F32)).reshape(
            2, HEAD_DIM // 2, 2 * PAGE_SIZE, 4 * CMP_HIDDEN).astype(BF16),
        'cmp_w2bd4': jnp.einsum('sed,bk->sbekd', p['cmp_w2'], jnp.eye(4, dtype=F32)).reshape(
            2, 4 * CMP_HIDDEN, 4 * HEAD_DIM).astype(BF16),
        'cmp_pe4': jnp.tile(jnp.einsum('scd,scde->se', p['cmp_pe'], p['cmp_w1'], precision=lax.Precision.HIGHEST), (1, 4)),
        'w_attn_out': p['w_attn_out'].astype(BF16),
        'w_o': p['w_o'].astype(BF16),
        'g_ffn': row(p['g_ffn']),
        'peer_wq': p['peer_wq'].astype(BF16),
        'peer_subkeys': p['peer_subkeys'].reshape(2 * PEER_HEADS, N_KEYS, PEER_KEY_DIM // 2).astype(BF16),
        'cmp_pe2': jnp.tile(jnp.einsum('scd,scde->se', p['cmp_pe'], p['cmp_w1'], precision=lax.Precision.HIGHEST), (1, 2)),
    }


def _pair_block_diag(a):
    z = jnp.zeros_like(a)
    return jnp.concatenate([jnp.concatenate([a, z], axis=-1), jnp.concatenate([z, a], axis=-1)], axis=-2)


CONV_HALO = 32
CONV_CHUNK = 64


def _ln_silu_project(y, lng, lnb, wout):
    mu = jnp.mean(y, axis=-1, keepdims=True)
    yc = y - mu
    var = jnp.mean(yc * yc, axis=-1, keepdims=True)
    yn = yc * lax.rsqrt(var + NORM_EPS) * lng + lnb
    act = yn * _sigmoid(yn)
    return _dot(act.astype(BF16), wout)


def _conv_prompt_kernel(glu_ref, cw_ref, cb_ref, lng_ref, lnb_ref, wout_ref, out_ref, ext_ref, y_ref):
    t = pl.program_id(1)
    tt = glu_ref.shape[1]

    @pl.when(t == 0)
    def _():
        ext_ref[0:CONV_HALO, :] = jnp.zeros((CONV_HALO, D_CONV), F32)

    @pl.when(t > 0)
    def _():
        ext_ref[0:CONV_HALO, :] = ext_ref[tt:tt + CONV_HALO, :]

    ext_ref[CONV_HALO:CONV_HALO + tt, :] = glu_ref[0]
    off = CONV_HALO - (CONV_WIDTH - 1)
    for r0 in range(0, tt, CONV_CHUNK):
        acc = jnp.zeros((CONV_CHUNK, D_CONV), F32)
        for j in range(CONV_WIDTH):
            acc = acc + ext_ref[r0 + off + j:r0 + off + j + CONV_CHUNK, :] * cw_ref[j:j + 1, :]
        y_ref[r0:r0 + CONV_CHUNK, :] = acc + cb_ref[...]
    out_ref[0] = _ln_silu_project(y_ref[...], lng_ref[...], lnb_ref[...], wout_ref[...])


def _conv_prompt(glu, w, t_tile):
    n, t, _ = glu.shape
    consts = (w['conv_w'], w['conv_b'], w['conv_ln_g'], w['conv_ln_b'], w['w_conv_out'])
    return pl.pallas_call(
        _conv_prompt_kernel,
        grid=(n, t // t_tile),
        in_specs=[pl.BlockSpec((1, t_tile, D_CONV), lambda i, j: (i, j, 0))] + [_const_spec(c.shape) for c in consts],
        out_specs=pl.BlockSpec((1, t_tile, D_MODEL), lambda i, j: (i, j, 0)),
        out_shape=jax.ShapeDtypeStruct((n, t, D_MODEL), F32),
        scratch_shapes=[pltpu.VMEM((t_tile + CONV_HALO, D_CONV), F32), pltpu.VMEM((t_tile, D_CONV), F32)],
        compiler_params=_params(("parallel", "arbitrary")),
        name="conv_prompt",
    )(glu, *consts)


def _conv_step_kernel(st_ref, gl_ref, cw_ref, cb_ref, lng_ref, lnb_ref, wout_ref, out_ref):
    n_buf = st_ref.shape[0]
    for t in range(gl_ref.shape[0]):
        acc = jnp.zeros(gl_ref.shape[1:], F32)
        for j in range(CONV_WIDTH):
            k = t + j
            row = st_ref[k] if k < n_buf else gl_ref[k - n_buf]
            acc = acc + row * cw_ref[j:j + 1, :]
        out_ref[t] = _ln_silu_project(acc + cb_ref[...], lng_ref[...], lnb_ref[...], wout_ref[...])


def _conv_step(st_tm, gl_tm, w):
    t, b, _ = gl_tm.shape
    consts = (w['conv_w'], w['conv_b'], w['conv_ln_g'], w['conv_ln_b'], w['w_conv_out'])
    return pl.pallas_call(
        _conv_step_kernel,
        grid=(1,),
        in_specs=[_const_spec(st_tm.shape), _const_spec(gl_tm.shape)] + [_const_spec(c.shape) for c in consts],
        out_specs=_const_spec((t, b, D_MODEL)),
        out_shape=jax.ShapeDtypeStruct((t, b, D_MODEL), F32),
        compiler_params=_params(("arbitrary",)),
        name="conv_step",
    )(st_tm, gl_tm, *consts)


BLOCK_COLS = CMP_BLOCK * KV_ROW
PAGE_SIZE = 128
PAGE_BLOCKS = PAGE_SIZE // CMP_BLOCK


def _gelu(x):
    return 0.5 * x * (1.0 + lax.erf(x * (2.0 ** -0.5)))


def _compress_rows(x_ref, m, w1_ref, pe_ref, w2_ref, kg_ref):
    outs = []
    for s in range(2):
        parts = []
        for hp in range(N_KV_HEADS // 2):
            acc = jnp.zeros((m, 2 * CMP_HIDDEN), F32)
            for c in range(CMP_BLOCK):
                col = c * KV_ROW + s * (KV_ROW // 2) + hp * LANES
                xs = x_ref[:, pl.ds(col, LANES)]
                acc = acc + _dot(xs.astype(BF16), w1_ref[s, c])
            hdn = _gelu(acc + pe_ref[s:s + 1, :])
            parts.append(_dot(hdn.astype(BF16), w2_ref[s]))
        outs.append(parts)
    ones_bd = _head_pair_ones()
    kc = []
    for part in outs[0]:
        hi, lo = _split_bf16(part * part)
        ss = _dot(hi, ones_bd) + _dot(lo, ones_bd)
        kc.append(part * lax.rsqrt(ss * (1.0 / HEAD_DIM) + NORM_EPS) * kg_ref[...])
    return jnp.concatenate(kc, axis=1), jnp.concatenate(outs[1], axis=1)


def _compress_prompt_kernel(x_ref, w1_ref, pe_ref, w2_ref, kg_ref, kc_ref, vc_ref):
    kc, vc = _compress_rows(x_ref, kc_ref.shape[0], w1_ref, pe_ref, w2_ref, kg_ref)
    kc_ref[...] = kc
    vc_ref[...] = vc


def _compress_prompt(ckv2d, w, m_tile):
    nblk = ckv2d.shape[0] // CMP_BLOCK
    x = ckv2d.reshape(nblk, BLOCK_COLS)
    consts = (w['cmp_w1bd'], w['cmp_pe2'], w['cmp_w2bd'], w['k_gain2'][0:1])
    out = pl.BlockSpec((m_tile, KV_ROW // 2), lambda i: (i, 0))
    return pl.pallas_call(
        _compress_prompt_kernel,
        grid=(nblk // m_tile,),
        in_specs=[pl.BlockSpec((m_tile, BLOCK_COLS), lambda i: (i, 0))] + [_const_spec(c.shape) for c in consts],
        out_specs=[out, out],
        out_shape=[jax.ShapeDtypeStruct((nblk, KV_ROW // 2), F32)] * 2,
        compiler_params=_params(("parallel",)),
        name="compress_prompt",
    )(x, *consts)


NSA_TQ = 256
CMP_PER_SEL = SEL_BLOCK // CMP_BLOCK
SEL_PER_TILE = NSA_TQ // SEL_BLOCK
WIN_TILES = WINDOW // NSA_TQ + 1
FLASH_ROWS = 256


def _dup_halves(x, in_head_half):
    return jnp.where(in_head_half, x, pltpu.roll(x, HEAD_DIM, 1))


def _rank_select(score, n_rows):
    row = lax.broadcasted_iota(jnp.int32, (n_rows, 1), 0)
    rank = jnp.zeros(score.shape, F32)
    for b in range(n_rows):
        sb = score[b:b + 1, :]
        tie_wins = jnp.where(row > b, 1.0, 0.0)
        rank = rank + jnp.where(sb > score, 1.0, jnp.where(sb == score, tie_wins, 0.0))
    return jnp.where(rank < SEL_TOPK, 1.0, 0.0)


def _flash_tile(q4, kd, vd, bias, m, l, acc):
    tq = bias.shape[0]
    lower = lax.broadcasted_iota(jnp.int32, (1, LANES), 1) < HEAD_DIM
    one = jnp.ones(vd.shape, BF16)
    v2 = jnp.concatenate([jnp.where(lower, vd, one), jnp.where(lower, one, vd)], axis=1)
    s = _dot_nt(q4, kd).reshape(GROUP, tq, vd.shape[0]) + bias[None]
    m_new = jnp.maximum(m, jnp.max(s, axis=-1, keepdims=True))
    alpha = jnp.exp(m - m_new)
    p = jnp.exp(s - m_new).astype(BF16).reshape(GROUP * tq, vd.shape[0])
    r = _dot(p, v2).reshape(GROUP, tq, 2 * LANES)
    sums = jnp.stack([r[g, :, HEAD_DIM:HEAD_DIM + 1] if g % 2 == 0 else r[g, :, LANES:LANES + 1] for g in range(GROUP)])
    new_acc = [acc[j] * jnp.where(lower, alpha[2 * j], alpha[2 * j + 1])
               + jnp.where(lower, r[2 * j, :, 0:LANES], r[2 * j + 1, :, LANES:2 * LANES]) for j in range(GROUP // 2)]
    return m_new, alpha * l + sums, new_acc


def _nsa_prompt_kernel(q_ref, kc_ref, vc_ref, ks_ref, vs_ref, kw0_ref, kw1_ref, kw2_ref, vw0_ref, vw1_ref, vw2_ref,
                       bg0_ref, bg1_ref, bg2_ref, o_ref):
    qi = pl.program_id(1)
    h = pl.program_id(2)
    tq = q_ref.shape[0]
    lane = lax.broadcasted_iota(jnp.int32, (1, LANES), 1)
    lower = lane < HEAD_DIM
    upper = lane >= HEAD_DIM
    in_head_half = (lane // HEAD_DIM) == (h % 2)
    pos = qi * tq + lax.broadcasted_iota(jnp.int32, (tq, 1), 0)

    q = q_ref[...]
    zero = jnp.zeros((tq, LANES), BF16)
    qm = [jnp.where(lower if g % 2 == 0 else upper, q[:, (g // 2) * LANES:(g // 2 + 1) * LANES], zero)
          for g in range(GROUP)]

    kcd = _dup_halves(kc_ref[...], in_head_half).astype(BF16)
    vcd = _dup_halves(vc_ref[...], in_head_half)
    vc_lo = jnp.where(lower, vcd, 0.0).astype(BF16)
    vc_hi = jnp.where(lower, 0.0, vcd).astype(BF16)
    nc = kc_ref.shape[0]
    blk_end = (lax.broadcasted_iota(jnp.int32, (1, nc), 1) + 1) * CMP_BLOCK - 1
    cmask = blk_end <= pos
    imp = jnp.zeros((tq, nc), F32)
    o_cmp = [jnp.zeros((tq, LANES), F32) for _ in range(GROUP // 2)]
    for g in range(GROUP):
        s = jnp.where(cmask, _dot_nt(qm[g], kcd), NEG_INF)
        e = jnp.where(cmask, jnp.exp(s - jnp.max(s, axis=-1, keepdims=True)), 0.0)
        pr = e / jnp.maximum(jnp.sum(e, axis=-1, keepdims=True), TINY)
        imp = imp + pr
        o_cmp[g // 2] = o_cmp[g // 2] + _dot(pr.astype(BF16), vc_lo if g % 2 == 0 else vc_hi)

    n_sel = nc // CMP_PER_SEL
    pair_t = jnp.where(lax.broadcasted_iota(jnp.int32, (n_sel, nc), 1) // CMP_PER_SEL
                       == lax.broadcasted_iota(jnp.int32, (n_sel, nc), 0), 1.0, 0.0).astype(BF16)
    imp_hi, imp_lo = _split_bf16(imp)
    imp_t = _dot_nt(pair_t, imp_hi) + _dot_nt(pair_t, imp_lo)
    pos_t = qi * tq + lax.broadcasted_iota(jnp.int32, (1, tq), 1)
    blk = lax.broadcasted_iota(jnp.int32, (n_sel, 1), 0)
    cur = pos_t // SEL_BLOCK
    forced = (blk == 0) | (blk == cur) | (blk == cur - 1)
    score = jnp.where(blk * SEL_BLOCK <= pos_t, imp_t + jnp.where(forced, FORCE_BONUS, 0.0), NEG_INF)
    sel_t = _rank_select(score, n_sel).astype(BF16)

    not_sel = (1.0 - sel_t.astype(F32)).astype(BF16)
    blk_of_key = lax.broadcasted_iota(jnp.int32, (n_sel, tq), 1) // SEL_BLOCK
    blk_row = lax.broadcasted_iota(jnp.int32, (n_sel, tq), 0)
    key_off = lax.broadcasted_iota(jnp.int32, (1, tq), 1)
    q4 = jnp.concatenate(qm, axis=0)

    def sel_body(kt, carry):
        m, l, acc = carry
        start = pl.multiple_of(kt * tq, tq)
        tags = jnp.where(blk_of_key + kt * SEL_PER_TILE == blk_row, NEG_INF, 0.0).astype(BF16)
        bias = _dot_tn(not_sel, tags) + jnp.where(kt * tq + key_off <= pos, 0.0, NEG_INF)
        return _flash_tile(q4, ks_ref[pl.ds(start, tq), :], vs_ref[pl.ds(start, tq), :], bias, m, l, acc)

    init = (jnp.full((GROUP, tq, 1), NEG_INF, F32), jnp.zeros((GROUP, tq, 1), F32),
            [jnp.zeros((tq, LANES), F32)] * (GROUP // 2))
    m, l, acc = lax.fori_loop(0, qi + 1, sel_body, init)
    o_sel = [acc[j] / jnp.where(lower, l[2 * j], l[2 * j + 1]) for j in range(GROUP // 2)]

    kpos = jnp.concatenate([(qi - d) * tq + key_off for d in range(WIN_TILES)], axis=1)
    diff = pos - kpos
    bias = jnp.where((diff >= 0) & (diff < WINDOW) & (kpos >= 0), 0.0, NEG_INF)
    kw = jnp.concatenate([kw0_ref[...], kw1_ref[...], kw2_ref[...]], axis=0)
    vw = jnp.concatenate([vw0_ref[...], vw1_ref[...], vw2_ref[...]], axis=0)
    m, l, acc = _flash_tile(q4, kw, vw, bias, *init)
    o_win = [acc[j] / jnp.where(lower, l[2 * j], l[2 * j + 1]) for j in range(GROUP // 2)]

    for j in range(GROUP // 2):
        sl = slice(j * LANES, (j + 1) * LANES)
        o = bg0_ref[:, sl] * o_cmp[j] + bg1_ref[:, sl] * o_sel[j] + bg2_ref[:, sl] * o_win[j]
        o_ref[:, sl] = o.astype(BF16)


def _nsa_prompt(q, kc, vc, kdup_s, vdup_s, kdup_w, vdup_w, bgx, n, t):
    tq = NSA_TQ
    nq = t // tq
    nc = t // CMP_BLOCK
    hw = GROUP * HEAD_DIM
    qspec = pl.BlockSpec((tq, hw), lambda i, j, h: (i * nq + j, h))
    cspec = pl.BlockSpec((nc, LANES), lambda i, j, h: (i, h // 2))
    seq_spec = pl.BlockSpec((t, LANES), lambda i, j, h: (i, h))
    win_specs = [pl.BlockSpec((tq, LANES), functools.partial(lambda d, i, j, h: (i * nq + jnp.maximum(j - d, 0), h), d))
                 for d in range(WIN_TILES)]
    bg_specs = [pl.BlockSpec((tq, hw), functools.partial(lambda b, i, j, h: (i * nq + j, b * N_KV_HEADS + h), b))
                for b in range(3)]
    return pl.pallas_call(
        _nsa_prompt_kernel,
        grid=(n, nq, N_KV_HEADS),
        in_specs=[qspec, cspec, cspec, seq_spec, seq_spec] + win_specs + win_specs + bg_specs,
        out_specs=qspec,
        out_shape=jax.ShapeDtypeStruct((n * t, C_Q), BF16),
        compiler_params=_params(("parallel", "parallel", "arbitrary")),
        name="nsa_prompt",
    )(q, kc, vc, kdup_s, vdup_s, kdup_w, kdup_w, kdup_w, vdup_w, vdup_w, vdup_w, bgx, bgx, bgx)


HD = N_KV_HEADS * HEAD_DIM
CP_PAGES = 32
BLOCKS_PER_PAGE = PAGE_SIZE // CMP_BLOCK
DEC_T = 4
STEP_ROWS = N_KV_HEADS * DEC_T * GROUP
SUB_PAGES = 32
PAGES_PER_SEQ = 64
NEW_ROWS = 8
LOWEST = -3.0e38


def _native_pool(cache_layer):
    n_pool = cache_layer.shape[0]
    return jnp.transpose(cache_layer, (0, 2, 3, 4, 1)).reshape(n_pool, 2, HD, PAGE_SIZE)


def _slab_copy(pool_ref, pt_ref, buf_ref, sem, seq, first_page, slot, i, s):
    page = pt_ref[seq, first_page + i]
    dst = buf_ref.at[slot, pl.ds((s * CP_PAGES + i) * HD, HD), :]
    return pltpu.make_async_copy(pool_ref.at[page, s], dst, sem.at[slot])


def _fetch_pages(pool_ref, pt_ref, buf_ref, sem, seq, first_page, slot, wait):
    for i in range(CP_PAGES):
        for s in range(2):
            cp = _slab_copy(pool_ref, pt_ref, buf_ref, sem, seq, first_page, slot, i, s)
            if wait:
                cp.wait()
            else:
                cp.start()


def _compress_pages_kernel(pt_ref, pool_ref, w1_hbm, pe_ref, w2_ref, kg_ref, kc_ref, vc_ref, buf_ref, w1_ref, sem, wsem):
    step = pl.program_id(0)
    n_steps = pl.num_programs(0)
    halves = pt_ref.shape[1] // CP_PAGES
    fetch = functools.partial(_fetch_pages, pool_ref, pt_ref, buf_ref, sem)

    @pl.when(step == 0)
    def _():
        cp = pltpu.make_async_copy(w1_hbm, w1_ref, wsem)
        cp.start()
        fetch(0, 0, 0, False)
        cp.wait()

    for slot in range(2):
        @pl.when(step % 2 == slot)
        def _():
            @pl.when(step + 1 < n_steps)
            def _():
                nxt = step + 1
                fetch(nxt // halves, (nxt % halves) * CP_PAGES, 1 - slot, False)

            fetch(step // halves, (step % halves) * CP_PAGES, slot, True)
            ones_bd = _head_pair_ones()
            for s, out_ref in ((0, kc_ref), (1, vc_ref)):
                acc = jnp.zeros((N_KV_HEADS * CP_PAGES, BLOCKS_PER_PAGE * CMP_HIDDEN), F32)
                for dp in range(HEAD_DIM // 2):
                    lanes = []
                    for d in (2 * dp, 2 * dp + 1):
                        rows = [buf_ref[slot, pl.ds(s * CP_PAGES * HD + h * HEAD_DIM + d, CP_PAGES, stride=HD), :]
                                for h in range(N_KV_HEADS)]
                        lanes.append(jnp.concatenate(rows, axis=0))
                    acc = acc + _dot(jnp.concatenate(lanes, axis=1).astype(BF16), w1_ref[s, dp])
                hdn = _gelu(acc + pe_ref[s:s + 1, :])
                out = _dot(hdn.astype(BF16), w2_ref[s])
                if s == 0:
                    tiles = []
                    for j in range(out.shape[1] // LANES):
                        part = out[:, j * LANES:(j + 1) * LANES]
                        hi, lo = _split_bf16(part * part)
                        ss = _dot(hi, ones_bd) + _dot(lo, ones_bd)
                        tiles.append(part * lax.rsqrt(ss * (1.0 / HEAD_DIM) + NORM_EPS) * kg_ref[...])
                    out = jnp.concatenate(tiles, axis=1)
                for h in range(N_KV_HEADS):
                    out_ref[0, h] = out[h * CP_PAGES:(h + 1) * CP_PAGES, :]


def _compress_pages(pool_t, page_table, w):
    n_seq, n_pages = page_table.shape
    halves = n_pages // CP_PAGES
    consts = (w['cmp_pe4'], w['cmp_w2bd4'], w['k_gain2'][0:1])
    out = pl.BlockSpec((1, N_KV_HEADS, CP_PAGES, BLOCKS_PER_PAGE * HEAD_DIM), lambda i, pt: (i // halves, 0, i % halves, 0))
    grid_spec = pltpu.PrefetchScalarGridSpec(
        num_scalar_prefetch=1,
        grid=(n_seq * halves,),
        in_specs=[pl.BlockSpec(memory_space=pl.ANY), pl.BlockSpec(memory_space=pl.ANY)]
        + [pl.BlockSpec(c.shape, functools.partial(lambda nd, i, pt: (0,) * nd, len(c.shape))) for c in consts],
        out_specs=[out, out],
        scratch_shapes=[pltpu.VMEM((2, 2 * CP_PAGES * HD, PAGE_SIZE), F32), pltpu.VMEM(w['cmp_w1pair'].shape, BF16),
                        pltpu.SemaphoreType.DMA((2,)), pltpu.SemaphoreType.DMA(())],
    )
    shape = jax.ShapeDtypeStruct((n_seq, N_KV_HEADS, n_pages, BLOCKS_PER_PAGE * HEAD_DIM), F32)
    return pl.pallas_call(
        _compress_pages_kernel,
        grid_spec=grid_spec,
        out_shape=[shape, shape],
        compiler_params=_params(("arbitrary",)),
        name="compress_pages",
    )(page_table, pool_t, w['cmp_w1pair'], *consts)


def _diag_heads(o):
    rows = STEP_ROWS // N_KV_HEADS
    parts = []
    for h in range(N_KV_HEADS):
        tile = o[h * rows:(h + 1) * rows, (h // 2) * LANES:(h // 2 + 1) * LANES]
        parts.append(pltpu.roll(tile, HEAD_DIM, 1) if h % 2 else tile)
    return jnp.concatenate(parts, axis=0)


def _topk_columns(score, score_new, idx, idx_new, k):
    sel = jnp.zeros(score.shape, F32)
    sel_new = jnp.zeros(score_new.shape, F32)
    big = jnp.float32(1.0e9)
    for _ in range(k):
        mx = jnp.maximum(jnp.max(score, axis=0, keepdims=True), score_new)
        first = jnp.minimum(jnp.min(jnp.where(score == mx, idx, big), axis=0, keepdims=True),
                            jnp.where(score_new == mx, idx_new, big))
        pick = idx == first
        pick_new = first == idx_new
        sel = jnp.where(pick, 1.0, sel)
        sel_new = jnp.where(pick_new, 1.0, sel_new)
        score = jnp.where(pick, LOWEST, score)
        score_new = jnp.where(pick_new, LOWEST, score_new)
    return sel, sel_new


def _nsa_step_kernel(pt_ref, q_ref, gate_ref, kc_ref, vc_ref, snew_ref, wnew_ref, wnewt_ref, win_ref, exp_ref, pool_ref,
                     x_ref, wout_ref, buf_ref, sem):
    seq = pl.program_id(0)
    n_seq = pl.num_programs(0)
    n_pages = pt_ref.shape[1]
    past_len = n_pages * PAGE_SIZE
    n_past_blk = past_len // SEL_BLOCK
    fetch = functools.partial(_fetch_pages, pool_ref, pt_ref, buf_ref, sem)

    @pl.when(seq == 0)
    def _():
        fetch(0, 0, 0, False)

    fetch(seq, CP_PAGES, 1, False)

    q = q_ref[0]
    row = lax.broadcasted_iota(jnp.int32, (STEP_ROWS, 1), 0)
    t_row = (row // GROUP) % DEC_T
    qpos = past_len + t_row
    lane = lax.broadcasted_iota(jnp.int32, (1, LANES), 1)
    lower = lane < HEAD_DIM
    rows_h = STEP_ROWS // N_KV_HEADS

    c_even = BLOCKS_PER_PAGE * (lane % PAGES_PER_SEQ) + 2 * (lane // PAGES_PER_SEQ)
    zero_q = jnp.zeros((STEP_ROWS, LANES), BF16)
    s_even = jnp.zeros((STEP_ROWS, LANES), F32)
    s_odd = jnp.zeros((STEP_ROWS, LANES), F32)
    kv_cat = []
    for h in range(N_KV_HEADS):
        tile = q[:, (h // 2) * LANES:(h // 2 + 1) * LANES]
        swapped = pltpu.roll(tile.astype(F32), HEAD_DIM, 1).astype(BF16)
        q_lo = jnp.where(lower, swapped if h % 2 else tile, zero_q)
        q_hi = jnp.where(lower, zero_q, tile if h % 2 else swapped)
        kcat = jnp.concatenate([kc_ref[0, h, :, 0:LANES], kc_ref[0, h, :, LANES:2 * LANES]], axis=0).astype(BF16)
        vcat = jnp.concatenate([vc_ref[0, h, :, 0:LANES], vc_ref[0, h, :, LANES:2 * LANES]], axis=0)
        kv_cat.append(vcat)
        s_even = s_even + _dot_nt(q_lo, kcat)
        s_odd = s_odd + _dot_nt(q_hi, kcat)
    m_even = (c_even + 1) * CMP_BLOCK - 1 <= qpos
    m_odd = (c_even + 2) * CMP_BLOCK - 1 <= qpos
    s_even = jnp.where(m_even, s_even, NEG_INF)
    s_odd = jnp.where(m_odd, s_odd, NEG_INF)
    mx = jnp.maximum(jnp.max(s_even, axis=-1, keepdims=True), jnp.max(s_odd, axis=-1, keepdims=True))
    e_even = jnp.where(m_even, jnp.exp(s_even - mx), 0.0)
    e_odd = jnp.where(m_odd, jnp.exp(s_odd - mx), 0.0)
    den = jnp.maximum(jnp.sum(e_even, axis=-1, keepdims=True) + jnp.sum(e_odd, axis=-1, keepdims=True), TINY)
    pr_even = e_even / den
    pr_odd = e_odd / den
    oc = jnp.zeros((STEP_ROWS, LANES), F32)
    for h in range(N_KV_HEADS):
        mine = (row // rows_h) == h
        oc = (oc + _dot(jnp.where(mine, pr_even, 0.0).astype(BF16), jnp.where(lower, kv_cat[h], 0.0).astype(BF16))
              + _dot(jnp.where(mine, pr_odd, 0.0).astype(BF16), jnp.where(lower, 0.0, kv_cat[h]).astype(BF16)))
    o_cmp = oc + pltpu.roll(oc, HEAD_DIM, 1)
    pr_sum = [pr_even + pr_odd]

    group_sum = jnp.where(lax.broadcasted_iota(jnp.int32, (STEP_ROWS, STEP_ROWS), 0) // GROUP
                          == lax.broadcasted_iota(jnp.int32, (STEP_ROWS, STEP_ROWS), 1) // GROUP, 1.0, 0.0).astype(BF16)
    p_hi, p_lo = _split_bf16(jnp.concatenate(pr_sum, axis=0))
    imp_t = _dot_tn(p_hi, group_sum) + _dot_tn(p_lo, group_sum)
    sub = lax.broadcasted_iota(jnp.int32, (LANES, 1), 0)
    blk = 2 * (sub % PAGES_PER_SEQ) + sub // PAGES_PER_SEQ
    col = lax.broadcasted_iota(jnp.int32, (1, STEP_ROWS), 1)
    qpos_c = past_len + (col // GROUP) % DEC_T
    cur = qpos_c // SEL_BLOCK
    forced = (blk == 0) | (blk == cur) | (blk == cur - 1)
    score = jnp.where(blk * SEL_BLOCK <= qpos_c, imp_t + jnp.where(forced, FORCE_BONUS, 0.0), NEG_INF)
    forced_new = (n_past_blk == cur) | (n_past_blk == cur - 1)
    score_new = jnp.where(n_past_blk * SEL_BLOCK <= qpos_c, jnp.where(forced_new, FORCE_BONUS, 0.0), NEG_INF)
    sel_t, sel_new_t = _topk_columns(score, score_new, blk.astype(F32), jnp.float32(n_past_blk), SEL_TOPK)
    bias_t = jnp.where(sel_t > 0.5, 0.0, NEG_INF).astype(BF16)
    flag_rows = jnp.where(lax.broadcasted_iota(jnp.int32, (NEW_ROWS, 1), 0) == 0, sel_new_t, 0.0).astype(BF16)
    sel_new = _dot_tn(flag_rows, jnp.ones((NEW_ROWS, LANES), BF16))[:, 0:1]

    def page_tiles(slot, page0, carry):
        m, l, acc = carry
        for sc in range(CP_PAGES // SUB_PAGES):
            pages = [sc * SUB_PAGES + i for i in range(0, SUB_PAGES, 2)]
            slab2 = lambda s, i: jnp.concatenate([buf_ref[slot, pl.ds((s * CP_PAGES + i) * HD, HD), :].astype(BF16),
                                                  buf_ref[slot, pl.ds((s * CP_PAGES + i + 1) * HD, HD), :].astype(BF16)], axis=1)
            s_t = jnp.concatenate([_dot(q, slab2(0, i)) for i in pages], axis=1)
            first_key = (page0 + sc * SUB_PAGES) * PAGE_SIZE
            s_t = s_t + _dot_tn(bias_t, exp_ref[:, first_key:first_key + SUB_PAGES * PAGE_SIZE])
            m_new = jnp.maximum(m, jnp.max(s_t, axis=-1, keepdims=True))
            alpha = jnp.exp(m - m_new)
            p = jnp.exp(s_t - m_new)
            l = alpha * l + jnp.sum(p, axis=-1, keepdims=True)
            acc = acc * alpha
            for k, i in enumerate(pages):
                acc = acc + _dot_nt(p[:, 2 * k * PAGE_SIZE:2 * (k + 1) * PAGE_SIZE].astype(BF16), slab2(1, i))
            m = m_new
        return m, l, acc

    carry = (jnp.full((STEP_ROWS, 1), NEG_INF, F32), jnp.zeros((STEP_ROWS, 1), F32), jnp.zeros((STEP_ROWS, HD), F32))
    fetch(seq, 0, 0, True)
    carry = page_tiles(0, 0, carry)

    @pl.when(seq + 1 < n_seq)
    def _():
        fetch(seq + 1, 0, 0, False)

    fetch(seq, CP_PAGES, 1, True)
    m, l, acc = page_tiles(1, CP_PAGES, carry)

    new_vis = lax.broadcasted_iota(jnp.int32, (1, NEW_ROWS), 1) <= t_row
    snew = snew_ref[0]
    s_n = jnp.where(new_vis & (sel_new > 0.5), _dot_nt(q, snew[:, 0:HD].astype(BF16)), NEG_INF)
    m_new = jnp.maximum(m, jnp.max(s_n, axis=-1, keepdims=True))
    alpha = jnp.exp(m - m_new)
    p_n = jnp.exp(s_n - m_new)
    l = alpha * l + jnp.sum(p_n, axis=-1, keepdims=True)
    acc = acc * alpha + _dot(p_n.astype(BF16), snew[:, HD:2 * HD].astype(BF16))
    o_sel = _diag_heads(acc / l)

    lb = win_ref.shape[3]
    kpos = past_len - lb + lax.broadcasted_iota(jnp.int32, (1, lb), 1)
    diff = qpos - kpos
    s_w = jnp.where((diff >= 0) & (diff < WINDOW) & (kpos >= 0), _dot(q, win_ref[0, 0].astype(BF16)), NEG_INF)
    wnew = wnew_ref[0]
    s_wn = jnp.where(new_vis, _dot_nt(q, wnew[:, 0:HD].astype(BF16)), NEG_INF)
    mw = jnp.maximum(jnp.max(s_w, axis=-1, keepdims=True), jnp.max(s_wn, axis=-1, keepdims=True))
    p_w = jnp.exp(s_w - mw)
    p_wn = jnp.exp(s_wn - mw)
    l_w = jnp.sum(p_w, axis=-1, keepdims=True) + jnp.sum(p_wn, axis=-1, keepdims=True)
    acc_w = _dot_nt(p_w.astype(BF16), win_ref[0, 1].astype(BF16)) + _dot(p_wn.astype(BF16), wnew[:, HD:2 * HD].astype(BF16))
    o_win = _diag_heads(acc_w / l_w)

    gate = gate_ref[0]
    x_ref[0] = gate[:, 0:1] * o_cmp + gate[:, 1:2] * o_sel + gate[:, 2:3] * o_win

    shift = LANES - DEC_T
    keep = lane < shift
    n_tiles = lb // LANES
    for s in range(2):
        for k in range(n_tiles):
            cur_tile = pltpu.roll(win_ref[0, s, :, k * LANES:(k + 1) * LANES], shift, 1)
            nxt_tile = (pltpu.roll(win_ref[0, s, :, (k + 1) * LANES:(k + 2) * LANES], shift, 1) if k + 1 < n_tiles
                        else wnewt_ref[0, s])
            wout_ref[0, s, :, k * LANES:(k + 1) * LANES] = jnp.where(keep, cur_tile, nxt_tile)


def _nsa_step(q_bd, gates, kc, vc, snew8, wnew8, wnew_t, win_t, expand, pool_t, page_table):
    n_seq, n_pages = page_table.shape
    lb = win_t.shape[3]
    per_seq = lambda shape: pl.BlockSpec((1,) + shape, functools.partial(lambda nd, i, pt: (i,) + (0,) * nd, len(shape)))
    cshape = (N_KV_HEADS, n_pages, BLOCKS_PER_PAGE * HEAD_DIM)
    grid_spec = pltpu.PrefetchScalarGridSpec(
        num_scalar_prefetch=1,
        grid=(n_seq,),
        in_specs=[per_seq((STEP_ROWS, HD)), per_seq((STEP_ROWS, LANES)), per_seq(cshape), per_seq(cshape),
                  per_seq((NEW_ROWS, KV_ROW)), per_seq((NEW_ROWS, KV_ROW)), per_seq((2, HD, LANES)), per_seq((2, HD, lb)),
                  pl.BlockSpec(expand.shape, lambda i, pt: (0, 0)), pl.BlockSpec(memory_space=pl.ANY)],
        out_specs=[per_seq((STEP_ROWS, LANES)), per_seq((2, HD, lb))],
        scratch_shapes=[pltpu.VMEM((2, 2 * CP_PAGES * HD, PAGE_SIZE), F32), pltpu.SemaphoreType.DMA((2,))],
    )
    return pl.pallas_call(
        _nsa_step_kernel,
        grid_spec=grid_spec,
        out_shape=[jax.ShapeDtypeStruct((n_seq, STEP_ROWS, LANES), F32), jax.ShapeDtypeStruct((n_seq, 2, HD, lb), F32)],
        compiler_params=_params(("arbitrary",)),
        name="nsa_step",
    )(page_table, q_bd, gates, kc, vc, snew8, wnew8, wnew_t, win_t, expand, pool_t)


def _step_query_layout(q_s, n_seq):
    q5 = q_s.reshape(n_seq, DEC_T, N_KV_HEADS, GROUP, HEAD_DIM)
    qt = jnp.transpose(q5, (0, 2, 1, 3, 4)).reshape(n_seq, N_KV_HEADS, DEC_T * GROUP, HEAD_DIM)
    eye = jnp.eye(N_KV_HEADS, dtype=q_s.dtype)
    return jnp.einsum('nhrd,hk->nhrkd', qt, eye).reshape(n_seq, STEP_ROWS, HD)


def _step_gate_layout(bgx_s, n_seq):
    bg = bgx_s.reshape(n_seq, DEC_T, 3, N_KV_HEADS, GROUP, HEAD_DIM)[..., 0]
    bg = jnp.transpose(bg, (0, 3, 1, 4, 2)).reshape(n_seq, STEP_ROWS, 3)
    return jnp.pad(bg, ((0, 0), (0, 0), (0, LANES - 3)))


def _step_output_layout(x, n_seq):
    x5 = x[:, :, :HEAD_DIM].reshape(n_seq, N_KV_HEADS, DEC_T, GROUP, HEAD_DIM)
    return jnp.transpose(x5, (0, 2, 1, 3, 4)).reshape(n_seq * DEC_T, C_Q)


def _block_expand_matrix(n_pages):
    rows = jnp.arange(LANES)
    keys = jnp.arange(n_pages * PAGE_SIZE)
    hit = (((rows[:, None] % PAGES_PER_SEQ) == (keys[None, :] // PAGE_SIZE))
           & ((rows[:, None] // PAGES_PER_SEQ) == ((keys[None, :] % PAGE_SIZE) // SEL_BLOCK)))
    return hit.astype(BF16)


PEER_HALF = PEER_KEY_DIM // 2
N_SIDES = 2 * PEER_HEADS


def _merge_kernel(x_ref, conv_ref, o_ref, gmix_ref, wmg_ref, wao_ref, wo_ref, gffn_ref, wq_ref, sk_ref,
                  h_ref, hn_ref, st_ref):
    x = x_ref[...]
    n = (x * lax.rsqrt(jnp.mean(x * x, axis=-1, keepdims=True) + NORM_EPS) * gmix_ref[...]).astype(BF16)
    mg = _sigmoid(_dot(n, wmg_ref[...]))
    attn = _dot(o_ref[...], wao_ref[...])
    mix = mg[:, :D_MODEL] * conv_ref[...] + mg[:, D_MODEL:] * attn
    h = x + _dot(mix.astype(BF16), wo_ref[...])
    h_ref[...] = h
    hn = (h * lax.rsqrt(jnp.mean(h * h, axis=-1, keepdims=True) + NORM_EPS) * gffn_ref[...]).astype(BF16)
    hn_ref[...] = hn
    qp = _dot(hn, wq_ref[...]).astype(BF16)
    for i in range(N_SIDES):
        st_ref[i * N_KEYS:(i + 1) * N_KEYS, :] = _dot_nt(sk_ref[i], qp[:, i * PEER_HALF:(i + 1) * PEER_HALF])


def _merge(x2d, conv_out, o, w, row_tile):
    ntok = x2d.shape[0]
    row = lambda width: pl.BlockSpec((row_tile, width), lambda i: (i, 0))
    consts = (w['g_mix'], w['w_mg'], w['w_attn_out'], w['w_o'], w['g_ffn'], w['peer_wq'], w['peer_subkeys'])
    return pl.pallas_call(
        _merge_kernel,
        grid=(ntok // row_tile,),
        in_specs=[row(D_MODEL), row(D_MODEL), row(C_Q)] + [_const_spec(c.shape) for c in consts],
        out_specs=[row(D_MODEL), row(D_MODEL), pl.BlockSpec((N_SIDES * N_KEYS, row_tile), lambda i: (0, i))],
        out_shape=[jax.ShapeDtypeStruct((ntok, D_MODEL), F32), jax.ShapeDtypeStruct((ntok, D_MODEL), BF16),
                   jax.ShapeDtypeStruct((N_SIDES * N_KEYS, ntok), F32)],
        compiler_params=_params(("parallel",)),
        name="merge",
    )(x2d, conv_out, o, *consts)


def _cand_groups():
    groups = []
    for j in range(PEER_TOPK):
        n_valid = PEER_TOPK // (j + 1)
        groups.append((j, n_valid, -(-n_valid // 8) * 8))
    return groups


def _top_rows_sorted(s, k):
    n = s.shape[0]
    row = lax.broadcasted_iota(jnp.int32, (n, 1), 0).astype(F32)
    out_row = lax.broadcasted_iota(jnp.int32, (k, 1), 0)
    vals = jnp.zeros((k, s.shape[1]), F32)
    idxs = jnp.zeros((k, s.shape[1]), F32)
    for r in range(k):
        mx = jnp.max(s, axis=0, keepdims=True)
        first = jnp.min(jnp.where(s == mx, row, float(n)), axis=0, keepdims=True)
        vals = jnp.where(out_row == r, mx, vals)
        idxs = jnp.where(out_row == r, first, idxs)
        s = jnp.where(row == first, LOWEST, s)
    return vals, idxs


def _peer_topk_kernel(st_ref, a_ref, b_ref, g_ref):
    tt = st_ref.shape[1]
    groups = _cand_groups()
    row16 = lax.broadcasted_iota(jnp.int32, (PEER_TOPK, 1), 0).astype(F32)
    a_rows, b_rows, g_rows = [], [], []
    for hd in range(PEER_HEADS):
        va, ia = _top_rows_sorted(st_ref[(2 * hd) * N_KEYS:(2 * hd + 1) * N_KEYS, :], PEER_TOPK)
        vb, ib = _top_rows_sorted(st_ref[(2 * hd + 1) * N_KEYS:(2 * hd + 2) * N_KEYS, :], PEER_TOPK)
        cands, flats = [], []
        for j, n_valid, n_rows in groups:
            i_idx = lax.broadcasted_iota(jnp.int32, (n_rows, 1), 0)
            cands.append(jnp.where(i_idx < n_valid, va[0:n_rows, :] + vb[j:j + 1, :], LOWEST))
            flats.append((i_idx * PEER_TOPK + j).astype(F32))
        cand = jnp.concatenate(cands, axis=0)
        flat = jnp.concatenate(flats, axis=0)
        n_flat = float(PEER_TOPK * PEER_TOPK)
        sc = jnp.zeros((PEER_TOPK, tt), F32)
        ea = jnp.zeros((PEER_TOPK, tt), F32)
        eb = jnp.zeros((PEER_TOPK, tt), F32)
        for r in range(PEER_TOPK):
            mx = jnp.max(cand, axis=0, keepdims=True)
            first = jnp.min(jnp.where(cand == mx, flat, n_flat), axis=0, keepdims=True)
            cand = jnp.where(flat == first, LOWEST, cand)
            i_sel = jnp.floor(first * (1.0 / PEER_TOPK))
            j_sel = first - i_sel * PEER_TOPK
            a_sel = jnp.sum(jnp.where(row16 == i_sel, ia, 0.0), axis=0, keepdims=True)
            b_sel = jnp.sum(jnp.where(row16 == j_sel, ib, 0.0), axis=0, keepdims=True)
            sc = jnp.where(row16 == r, mx, sc)
            ea = jnp.where(row16 == r, a_sel, ea)
            eb = jnp.where(row16 == r, b_sel, eb)
        e = jnp.exp(sc - sc[0:1, :])
        g_rows.append(e / jnp.sum(e, axis=0, keepdims=True))
        a_rows.append(ea)
        b_rows.append(eb)
    a_ref[...] = jnp.transpose(jnp.concatenate(a_rows, axis=0))
    b_ref[...] = jnp.transpose(jnp.concatenate(b_rows, axis=0))
    g_ref[...] = jnp.transpose(jnp.concatenate(g_rows, axis=0))


def _peer_topk(scores_t, tok_tile):
    ntok = scores_t.shape[1]
    n_pick = PEER_HEADS * PEER_TOPK
    out = pl.BlockSpec((tok_tile, n_pick), lambda i: (i, 0))
    return pl.pallas_call(
        _peer_topk_kernel,
        grid=(ntok // tok_tile,),
        in_specs=[pl.BlockSpec((N_SIDES * N_KEYS, tok_tile), lambda i: (0, i))],
        out_specs=[out, out, out],
        out_shape=[jax.ShapeDtypeStruct((ntok, n_pick), F32)] * 3,
        compiler_params=_params(("parallel",)),
        name="peer_topk",
    )(scores_t)


W_TOK = 64


def _peer_w_kernel(a_ref, b_ref, g_ref, w_ref):
    sub = lax.broadcasted_iota(jnp.int32, (N_KEYS, 1), 0).astype(F32)

    def body(t, carry):
        a_row = a_ref[pl.ds(t, 1), :]
        b_row = b_ref[pl.ds(t, 1), :]
        g_row = g_ref[pl.ds(t, 1), :]
        ga = jnp.where(sub == a_row, g_row, 0.0).astype(BF16)
        ob = jnp.where(sub == b_row, 1.0, 0.0).astype(BF16)
        w_ref[t] = _dot_nt(ga, ob)
        return carry

    lax.fori_loop(0, W_TOK, body, 0, unroll=32)


def _peer_weights(a_idx, b_idx, gate):
    ntok, n_pick = a_idx.shape
    row = pl.BlockSpec((W_TOK, n_pick), lambda i: (i, 0))
    return pl.pallas_call(
        _peer_w_kernel,
        grid=(ntok // W_TOK,),
        in_specs=[row, row, row],
        out_specs=pl.BlockSpec((W_TOK, N_KEYS, N_KEYS), lambda i: (i, 0, 0)),
        out_shape=jax.ShapeDtypeStruct((ntok, N_KEYS, N_KEYS), F32),
        compiler_params=_params(("parallel",)),
        name="peer_weights",
    )(a_idx, b_idx, gate)


PEER_TOK = 512
PEER_ABLK = 16


def _peer_dense_kernel(hn_ref, h_ref, w_ref, u_ref, v_ref, y_ref, acc_ref):
    j = pl.program_id(1)

    @pl.when(j == 0)
    def _():
        acc_ref[...] = jnp.zeros(acc_ref.shape, F32)

    hn = hn_ref[...]
    pair = 2 * N_KEYS
    for k in range(PEER_ABLK // 2):
        act = _dot_nt(hn, u_ref[k * pair:(k + 1) * pair, :])
        wk = jnp.concatenate([w_ref[:, 2 * k, :], w_ref[:, 2 * k + 1, :]], axis=1)
        acc_ref[...] += _dot((wk * _gelu(act)).astype(BF16), v_ref[k * pair:(k + 1) * pair, :])

    @pl.when(j == pl.num_programs(1) - 1)
    def _():
        y_ref[...] = h_ref[...] + acc_ref[...]


def _peer_dense(hn, h, w_atb, u, v):
    ntok = hn.shape[0]
    blk = PEER_ABLK * N_KEYS
    tok = lambda: pl.BlockSpec((PEER_TOK, D_MODEL), lambda i, j: (i, 0))
    exp = lambda: pl.BlockSpec((blk, D_MODEL), lambda i, j: (j, 0))
    return pl.pallas_call(
        _peer_dense_kernel,
        grid=(ntok // PEER_TOK, N_KEYS // PEER_ABLK),
        in_specs=[tok(), tok(), pl.BlockSpec((PEER_TOK, PEER_ABLK, N_KEYS), lambda i, j: (i, j, 0)), exp(), exp()],
        out_specs=tok(),
        out_shape=jax.ShapeDtypeStruct((ntok, D_MODEL), F32),
        scratch_shapes=[pltpu.VMEM((PEER_TOK, D_MODEL), F32)],
        compiler_params=_params(("parallel", "arbitrary")),
        name="peer_dense",
    )(hn, h, w_atb, u, v)


def kernel(x_prompt, x_sample, cache_cmp_kv, cache_sel_kv, state_win_kv, state_conv, page_table, g_mix, w_in, conv_w, conv_b, conv_ln_g, conv_ln_b, w_conv_out, q_gain, k_gain, cmp_pe, cmp_w1, cmp_w2, w_attn_out, w_o, g_ffn, peer_wq, peer_subkeys, peer_u, peer_v):
    layer = 0
    p = {'g_mix': g_mix, 'w_in': w_in, 'conv_w': conv_w, 'conv_b': conv_b, 'conv_ln_g': conv_ln_g,
         'conv_ln_b': conv_ln_b, 'w_conv_out': w_conv_out, 'q_gain': q_gain, 'k_gain': k_gain, 'cmp_pe': cmp_pe,
         'cmp_w1': cmp_w1, 'cmp_w2': cmp_w2, 'w_attn_out': w_attn_out, 'w_o': w_o, 'g_ffn': g_ffn,
         'peer_wq': peer_wq, 'peer_subkeys': peer_subkeys}
    w = _prep_weights({k: v[layer] for k, v in p.items()})
    u_bf = peer_u[layer].astype(BF16)
    v_bf = peer_v[layer].astype(BF16)

    def ffn(x2d, conv_out, o):
        h, hn, scores_t = _merge(x2d, conv_out, o, w, ROW_TILE)
        a_idx, b_idx, gate = _peer_topk(scores_t, ROW_TILE)
        return _peer_dense(hn, h, _peer_weights(a_idx, b_idx, gate), u_bf, v_bf)

    n, t, _ = x_prompt.shape
    xp = x_prompt.reshape(n * t, D_MODEL)
    cos_p, sin_p = _rope_tables(jnp.arange(t, dtype=jnp.int32))
    glu, q, ckv, _, _, bgx, kds, vds, kdw, vdw, ckv_t, skv_t, wkv_t = _project(xp, cos_p, sin_p, w, ROW_TILE, t)
    glu3 = glu.reshape(n, t, D_CONV)
    conv_p = _conv_prompt(glu3, w, ROW_TILE).reshape(n * t, D_MODEL)
    kc, vc = _compress_prompt(ckv, w, 128)
    o_p = _nsa_prompt(q, kc, vc, kds, vds, kdw, vdw, bgx, n, t)
    y_p = ffn(xp, conv_p, o_p).reshape(n, t, D_MODEL)

    n_s, t_s, _ = x_sample.shape
    past_len = page_table.shape[1] * PAGE_SIZE
    xs = x_sample.reshape(n_s * t_s, D_MODEL)
    pos_s = past_len + jnp.arange(t_s, dtype=jnp.int32)
    cos_s, sin_s = _rope_tables(jnp.tile(pos_s, ROW_TILE // t_s))
    glu_s, q_s, ckv_s, skv_s, wkv_s, bgx_s = _project(xs, cos_s, sin_s, w, ROW_TILE, ROW_TILE)[:6]
    glu_s3 = glu_s.reshape(n_s, t_s, D_CONV)
    conv_s = _conv_step(jnp.swapaxes(state_conv[layer], 0, 1), jnp.swapaxes(glu_s3, 0, 1), w)
    conv_s = jnp.swapaxes(conv_s, 0, 1).reshape(n_s * t_s, D_MODEL)
    assert t_s == DEC_T and page_table.shape[1] == PAGES_PER_SEQ
    kc_s, vc_s = _compress_pages(_native_pool(cache_cmp_kv[layer]), page_table, w)
    lb = state_win_kv.shape[2]
    win_t = jnp.transpose(state_win_kv[layer], (0, 2, 3, 4, 1)).reshape(n_s, 2, HD, lb)
    pad_new = lambda a: jnp.pad(a.reshape(n_s, t_s, KV_ROW), ((0, 0), (0, NEW_ROWS - t_s), (0, 0)))
    wnew_t = jnp.pad(jnp.transpose(wkv_s.reshape(n_s, t_s, 2, HD), (0, 2, 3, 1)), ((0, 0), (0, 0), (0, 0), (LANES - t_s, 0)))
    x_step, win_s = _nsa_step(_step_query_layout(q_s, n_s), _step_gate_layout(bgx_s, n_s), kc_s, vc_s,
                              pad_new(skv_s), pad_new(wkv_s), wnew_t, win_t, _block_expand_matrix(PAGES_PER_SEQ),
                              _native_pool(cache_sel_kv[layer]), page_table)
    o_s = _step_output_layout(x_step, n_s).astype(BF16)
    y_s = ffn(xs, conv_s, o_s).reshape(n_s, t_s, D_MODEL)
    win_s = jnp.transpose(win_s.reshape(n_s, 2, N_KV_HEADS, HEAD_DIM, lb), (0, 4, 1, 2, 3))

    kv6 = lambda a, nn, tt: a.reshape(1, nn, tt, 2, N_KV_HEADS, HEAD_DIM)
    kv6t = lambda a: jnp.transpose(a.reshape(a.shape[0], 2, N_KV_HEADS, HEAD_DIM, a.shape[2]), (0, 4, 1, 2, 3))[None]
    n_win = min(WINDOW, t)
    new_conv_s = jnp.concatenate([state_conv[layer][:, t_s:], glu_s3], axis=1)
    return (y_p, y_s,
            kv6t(ckv_t), kv6t(skv_t), kv6t(wkv_t[:, :, t - n_win:]), glu3[None, :, t - (CONV_WIDTH - 1):],
            kv6(ckv_s, n_s, t_s), kv6(skv_s, n_s, t_s), win_s[None], new_conv_s[None])
```
